```python
import jax, jax.numpy as jnp
from jax import lax
import numpy as np

D_MODEL = 1024
BATCH = 2
SEQ = 16384
DEPTH = 2
DEC_BATCH = 4
DEC_SEQ = 4096
PAST_LEN = 128

GRID_W = 64
N_MIXERS = 2
N_MLA_LAYERS = (DEPTH + 1) // 2
N_NA_LAYERS = DEPTH // 2
MLA_HEADS = 16
Q_LORA = 384
KV_LORA = 256
QK_NOPE = 128
QK_ROPE = 64
V_HEAD = 128
ROPE_THETA = 10000.0
Q_BLOCK = 128
NA_HEADS = 16
NA_HEAD_DIM = D_MODEL // NA_HEADS
NA_KH = 8
NA_KW = 16
D_FF = 4 * D_MODEL
EPS = 1e-6

kernel_name = "hybrid_mla_natten_encoder"


def rms_norm(x, g):
    xf = x.astype(jnp.float32)
    y = xf * lax.rsqrt(jnp.mean(xf * xf, axis=-1, keepdims=True) + EPS)
    return (y * g.astype(jnp.float32)).astype(x.dtype)


def rope_tables(s, dtype):
    inv = ROPE_THETA ** (-jnp.arange(0, QK_ROPE, 2, dtype=jnp.float32) / QK_ROPE)
    ang = jnp.arange(s, dtype=jnp.float32)[:, None] * inv[None, :]
    return jnp.cos(ang).astype(dtype), jnp.sin(ang).astype(dtype)


def apply_rope(x, cos, sin):
    x1, x2 = jnp.split(x, 2, axis=-1)
    return jnp.concatenate([x1 * cos - x2 * sin, x1 * sin + x2 * cos], axis=-1)


def mla(x, w_dq, q_norm, w_uq, w_dkv, kv_norm, w_ukv, w_o):
    b, s, _ = x.shape
    cos, sin = rope_tables(s, x.dtype)
    c_q = rms_norm(x @ w_dq, q_norm)
    q = (c_q @ w_uq).reshape(b, s, MLA_HEADS, QK_NOPE + QK_ROPE)
    q_nope = q[..., :QK_NOPE]
    q_rope = apply_rope(q[..., QK_NOPE:], cos[:, None], sin[:, None])
    kv_a = x @ w_dkv
    c_kv = rms_norm(kv_a[..., :KV_LORA], kv_norm)
    k_rope = apply_rope(kv_a[..., KV_LORA:], cos, sin)
    kv = (c_kv @ w_ukv).reshape(b, s, MLA_HEADS, QK_NOPE + V_HEAD)
    k_nope, v = kv[..., :QK_NOPE], kv[..., QK_NOPE:]
    scale = (QK_NOPE + QK_ROPE) ** -0.5
    nb = s // Q_BLOCK

    def q_block(args):
        qn, qr = args
        sc = jnp.einsum('bqhd,bkhd->bhqk', qn, k_nope) + jnp.einsum('bqhd,bkd->bhqk', qr, k_rope)
        p = jax.nn.softmax(sc.astype(jnp.float32) * scale, axis=-1).astype(v.dtype)
        return jnp.einsum('bhqk,bkhd->bqhd', p, v)

    def to_blocks(t):
        return jnp.moveaxis(t.reshape(b, nb, Q_BLOCK, *t.shape[2:]), 1, 0)

    o = lax.map(q_block, (to_blocks(q_nope), to_blocks(q_rope)))
    o = jnp.moveaxis(o, 0, 1).reshape(b, s, MLA_HEADS * V_HEAD)
    return o @ w_o


def neighbourhood_attention(x, w_qkv, rpb, w_o):
    b, s, _ = x.shape
    rows = s // GRID_W
    kh = min(NA_KH, rows)
    qkv = (x @ w_qkv).reshape(b, rows, GRID_W, 3, NA_HEADS, NA_HEAD_DIM)
    q, k, v = qkv[:, :, :, 0], qkv[:, :, :, 1], qkv[:, :, :, 2]
    cols = np.arange(GRID_W)
    col_start = np.clip(cols - NA_KW // 2, 0, GRID_W - NA_KW)
    col_idx_np = col_start[:, None] + np.arange(NA_KW)[None, :]
    col_idx = jnp.asarray(col_idx_np, dtype=jnp.int32)
    rel_col = jnp.asarray(col_idx_np - cols[:, None] + NA_KW - 1, dtype=jnp.int32)
    scale = NA_HEAD_DIM ** -0.5

    def row_block(args):
        r, q_r = args
        r0 = jnp.clip(r - kh // 2, 0, rows - kh)
        k_rows = lax.dynamic_slice_in_dim(k, r0, kh, axis=1)
        v_rows = lax.dynamic_slice_in_dim(v, r0, kh, axis=1)
        k_win = k_rows[:, :, col_idx]
        v_win = v_rows[:, :, col_idx]
        rel_row = r0 + jnp.arange(kh, dtype=jnp.int32) - r + NA_KH - 1
        bias = rpb[:, rel_row[None, :, None], rel_col[:, None, :]]
        sc = (jnp.einsum('bqhd,biqjhd->bhqij', q_r, k_win).astype(jnp.float32) * scale
              + bias.astype(jnp.float32)[None])
        p = jax.nn.softmax(sc.reshape(b, NA_HEADS, GRID_W, kh * NA_KW), axis=-1)
        p = p.reshape(sc.shape).astype(v.dtype)
        return jnp.einsum('bhqij,biqjhd->bqhd', p, v_win)

    o = lax.map(row_block, (jnp.arange(rows, dtype=jnp.int32), jnp.moveaxis(q, 1, 0)))
    o = jnp.moveaxis(o, 0, 1).reshape(b, s, NA_HEADS * NA_HEAD_DIM)
    return o @ w_o


def sq_relu_mlp(x, w1, w2):
    h = jax.nn.relu(x @ w1)
    return (h * h) @ w2


def trunk(x, attn_norm, mlp_norm, final_norm, mla_w_dq, mla_q_norm, mla_w_uq, mla_w_dkv,
          mla_kv_norm, mla_w_ukv, mla_w_o, na_w_qkv, na_rpb, na_w_o, mlp_w1, mlp_w2):
    for i in range(DEPTH):
        h = rms_norm(x, attn_norm[i])
        j = i // N_MIXERS
        if i % N_MIXERS == 0:
            h = mla(h, mla_w_dq[j], mla_q_norm[j], mla_w_uq[j], mla_w_dkv[j],
                    mla_kv_norm[j], mla_w_ukv[j], mla_w_o[j])
        else:
            h = neighbourhood_attention(h, na_w_qkv[j], na_rpb[j], na_w_o[j])
        x = x + h
        x = x + sq_relu_mlp(rms_norm(x, mlp_norm[i]), mlp_w1[i], mlp_w2[i])
    return rms_norm(x, final_norm)


def setup_inputs(seed: int = 0) -> dict:
    key = jax.random.key(seed)
    ks = jax.random.split(key, 20)
    f32 = jnp.float32

    def nrm(k, shape, scale):
        return jax.random.normal(k, shape, f32) * scale

    def gain(k, shape):
        return 1.0 + 0.01 * jax.random.normal(k, shape, f32)

    return {
        "x_prompt": jax.random.normal(ks[0], (BATCH, SEQ, D_MODEL), f32),
        "x_sample": jax.random.normal(ks[1], (DEC_BATCH, DEC_SEQ, D_MODEL), f32),
        "attn_norm": gain(ks[2], (DEPTH, D_MODEL)),
        "mlp_norm": gain(ks[3], (DEPTH, D_MODEL)),
        "final_norm": gain(ks[4], (D_MODEL,)),
        "mla_w_dq": nrm(ks[5], (N_MLA_LAYERS, D_MODEL, Q_LORA), D_MODEL ** -0.5),
        "mla_q_norm": gain(ks[6], (N_MLA_LAYERS, Q_LORA)),
        "mla_w_uq": nrm(ks[7], (N_MLA_LAYERS, Q_LORA, MLA_HEADS * (QK_NOPE + QK_ROPE)), Q_LORA ** -0.5),
        "mla_w_dkv": nrm(ks[8], (N_MLA_LAYERS, D_MODEL, KV_LORA + QK_ROPE), D_MODEL ** -0.5),
        "mla_kv_norm": gain(ks[9], (N_MLA_LAYERS, KV_LORA)),
        "mla_w_ukv": nrm(ks[10], (N_MLA_LAYERS, KV_LORA, MLA_HEADS * (QK_NOPE + V_HEAD)), KV_LORA ** -0.5),
        "mla_w_o": nrm(ks[11], (N_MLA_LAYERS, MLA_HEADS * V_HEAD, D_MODEL), (MLA_HEADS * V_HEAD) ** -0.5),
        "na_w_qkv": nrm(ks[12], (N_NA_LAYERS, D_MODEL, 3 * NA_HEADS * NA_HEAD_DIM), D_MODEL ** -0.5),
        "na_rpb": nrm(ks[13], (N_NA_LAYERS, NA_HEADS, 2 * NA_KH - 1, 2 * NA_KW - 1), 0.02),
        "na_w_o": nrm(ks[14], (N_NA_LAYERS, NA_HEADS * NA_HEAD_DIM, D_MODEL), (NA_HEADS * NA_HEAD_DIM) ** -0.5),
        "mlp_w1": nrm(ks[15], (DEPTH, D_MODEL, D_FF), D_MODEL ** -0.5),
        "mlp_w2": nrm(ks[16], (DEPTH, D_FF, D_MODEL), D_FF ** -0.5),
    }


def reference(x_prompt, x_sample, attn_norm, mlp_norm, final_norm, mla_w_dq, mla_q_norm, mla_w_uq,
              mla_w_dkv, mla_kv_norm, mla_w_ukv, mla_w_o, na_w_qkv, na_rpb, na_w_o, mlp_w1, mlp_w2):
    y_prompt = trunk(x_prompt, attn_norm, mlp_norm, final_norm, mla_w_dq, mla_q_norm, mla_w_uq,
                     mla_w_dkv, mla_kv_norm, mla_w_ukv, mla_w_o, na_w_qkv, na_rpb, na_w_o, mlp_w1, mlp_w2)
    y_sample = trunk(x_sample, attn_norm, mlp_norm, final_norm, mla_w_dq, mla_q_norm, mla_w_uq,
                     mla_w_dkv, mla_kv_norm, mla_w_ukv, mla_w_o, na_w_qkv, na_rpb, na_w_o, mlp_w1, mlp_w2)
    return (y_prompt, y_sample)
```

```python
import functools
import math

import numpy as np
import jax
import jax.numpy as jnp
from jax import lax
from jax.experimental import pallas as pl
from jax.experimental.pallas import tpu as pltpu

D_MODEL = 1024
GRID_W = 64
MLA_HEADS = 16
Q_LORA = 384
KV_LORA = 256
QK_NOPE = 128
QK_ROPE = 64
V_HEAD = 128
ROPE_THETA = 10000.0
NA_HEADS = 16
NA_HEAD_DIM = D_MODEL // NA_HEADS
NA_KH = 8
NA_KW = 16
D_FF = 4 * D_MODEL
EPS = 1e-6

LANES = 128
QK_PAD = 2 * LANES
NA_PAIRS = NA_HEADS * NA_HEAD_DIM // LANES
NA_WIN = NA_KH * GRID_W
MASK_VALUE = -1e30
VMEM_LIMIT = 56 * 1024 * 1024

F32 = jnp.float32
BF16 = jnp.bfloat16


def _rms(x, g):
    return x * lax.rsqrt(jnp.mean(x * x, axis=-1, keepdims=True) + EPS) * g


def _const_spec(shape):
    zeros = (0,) * len(shape)
    return pl.BlockSpec(shape, lambda *_: zeros, pipeline_mode=pl.Buffered(1))


def _params(n_axes):
    return pltpu.CompilerParams(
        dimension_semantics=("arbitrary",) * n_axes, vmem_limit_bytes=VMEM_LIMIT)


def _mla_proj_kernel(x_ref, g_ref, wdq_ref, qn_ref, wuq_ref, wdkvc_ref, wkr_ref, kvn_ref,
                     wukT_ref, wuv_ref, tq0_ref, tq1_ref, tk0_ref, tk1_ref,
                     q_ref, kT_ref, v_ref, *, q_scale):
    h = _rms(x_ref[0], g_ref[...]).astype(BF16)
    cq = jnp.dot(h, wdq_ref[...], preferred_element_type=F32)
    cq = _rms(cq, qn_ref[...]).astype(BF16)
    q = jnp.dot(cq, wuq_ref[...], preferred_element_type=F32)
    tq0 = tq0_ref[...]
    tq1 = tq1_ref[...]
    for hd in range(MLA_HEADS):
        base = hd * QK_PAD
        nope = q[:, base:base + LANES] * q_scale
        r = q[:, base + LANES:base + QK_PAD]
        roped = r * tq0 + pltpu.roll(r, LANES // 2, axis=1) * tq1
        q_ref[0, :, base:base + LANES] = nope.astype(BF16)
        q_ref[0, :, base + LANES:base + QK_PAD] = roped.astype(BF16)

    ckv = jnp.dot(h, wdkvc_ref[...], preferred_element_type=F32)
    ckv = _rms(ckv, kvn_ref[...]).astype(BF16)
    kr = jnp.dot(h, wkr_ref[...], preferred_element_type=F32)
    kr = kr * tk0_ref[...] + pltpu.roll(kr, LANES // 2, axis=1) * tk1_ref[...]
    krT = kr.T[:QK_ROPE].astype(BF16)
    knT = lax.dot_general(wukT_ref[...], ckv, (((1,), (1,)), ((), ())),
                          preferred_element_type=F32).astype(BF16)
    zpad = jnp.zeros((QK_PAD - QK_NOPE - QK_ROPE, krT.shape[1]), BF16)
    for hd in range(MLA_HEADS):
        base = hd * QK_PAD
        kT_ref[0, base:base + QK_NOPE, :] = knT[hd * QK_NOPE:(hd + 1) * QK_NOPE]
        kT_ref[0, base + QK_NOPE:base + QK_NOPE + QK_ROPE, :] = krT
        kT_ref[0, base + QK_NOPE + QK_ROPE:base + QK_PAD, :] = zpad
    v_ref[0] = jnp.dot(ckv, wuv_ref[...], preferred_element_type=F32).astype(BF16)


def _mla_proj(x, g, w_dq, q_norm, w_uq, w_dkv, kv_norm, w_ukv, *, tm):
    b, s, _ = x.shape
    q_scale = (QK_NOPE + QK_ROPE) ** -0.5 * math.log2(math.e)

    half = QK_ROPE // 2
    wq = w_uq.reshape(Q_LORA, MLA_HEADS, QK_NOPE + QK_ROPE)
    x1, x2 = wq[..., QK_NOPE:QK_NOPE + half], wq[..., QK_NOPE + half:]
    wuq_pad = jnp.concatenate([wq[..., :QK_NOPE], x1, x2, x2, x1], axis=-1)
    wuq_pad = wuq_pad.reshape(Q_LORA, MLA_HEADS * QK_PAD).astype(BF16)
    wdkv_c = w_dkv[:, :KV_LORA].astype(BF16)
    k1, k2 = w_dkv[:, KV_LORA:KV_LORA + half], w_dkv[:, KV_LORA + half:]
    w_kr = jnp.concatenate([k1, k2, k2, k1], axis=-1).astype(BF16)
    wkv = w_ukv.reshape(KV_LORA, MLA_HEADS, QK_NOPE + V_HEAD)
    w_ukT = wkv[..., :QK_NOPE].reshape(KV_LORA, MLA_HEADS * QK_NOPE).T.astype(BF16)
    w_uv = wkv[..., QK_NOPE:].reshape(KV_LORA, MLA_HEADS * V_HEAD).astype(BF16)

    inv = ROPE_THETA ** (-jnp.arange(0, QK_ROPE, 2, dtype=F32) / QK_ROPE)
    ang = jnp.arange(s, dtype=F32)[:, None] * inv[None, :]
    cos, sin = jnp.cos(ang), jnp.sin(ang)
    zero = jnp.zeros_like(cos)
    t0 = jnp.concatenate([cos, cos, zero, zero], axis=-1)
    t1 = jnp.concatenate([-sin, sin, zero, zero], axis=-1)

    grid = (b, s // tm)
    row = lambda bi, i: (bi, i, 0)
    tab = pl.BlockSpec((tm, LANES), lambda bi, i: (i, 0))
    return pl.pallas_call(
        functools.partial(_mla_proj_kernel, q_scale=q_scale),
        grid=grid,
        in_specs=[
            pl.BlockSpec((1, tm, D_MODEL), row),
            _const_spec((1, D_MODEL)),
            _const_spec((D_MODEL, Q_LORA)),
            _const_spec((1, Q_LORA)),
            _const_spec((Q_LORA, MLA_HEADS * QK_PAD)),
            _const_spec((D_MODEL, KV_LORA)),
            _const_spec((D_MODEL, LANES)),
            _const_spec((1, KV_LORA)),
            _const_spec((MLA_HEADS * QK_NOPE, KV_LORA)),
            _const_spec((KV_LORA, MLA_HEADS * V_HEAD)),
            tab, tab, tab, tab,
        ],
        out_specs=[
            pl.BlockSpec((1, tm, MLA_HEADS * QK_PAD), row),
            pl.BlockSpec((1, MLA_HEADS * QK_PAD, tm), lambda bi, i: (bi, 0, i)),
            pl.BlockSpec((1, tm, MLA_HEADS * V_HEAD), row),
        ],
        out_shape=[
            jax.ShapeDtypeStruct((b, s, MLA_HEADS * QK_PAD), BF16),
            jax.ShapeDtypeStruct((b, MLA_HEADS * QK_PAD, s), BF16),
            jax.ShapeDtypeStruct((b, s, MLA_HEADS * V_HEAD), BF16),
        ],
        compiler_params=_params(2),
        name="mla_proj",
    )(x, g.reshape(1, -1), w_dq.astype(BF16), q_norm.reshape(1, -1), wuq_pad, wdkv_c, w_kr,
      kv_norm.reshape(1, -1), w_ukT, w_uv, t0 * q_scale, t1 * q_scale, t0, t1)


def _mla_attn_kernel(q_ref, kT_ref, v_ref, o_ref, m_scr, l_scr, acc_scr, *, tk):
    s_len = v_ref.shape[1]
    q = q_ref[0]
    m_scr[...] = jnp.full(m_scr.shape, MASK_VALUE, F32)
    l_scr[...] = jnp.zeros(l_scr.shape, F32)
    acc_scr[...] = jnp.zeros(acc_scr.shape, F32)

    def body(j, carry):
        off = pl.multiple_of(j * tk, tk)
        s = jnp.dot(q, kT_ref[0, :, pl.ds(off, tk)], preferred_element_type=F32)
        tiles = [s[:, c * LANES:(c + 1) * LANES] for c in range(tk // LANES)]
        m_lane = functools.reduce(jnp.maximum, tiles)
        m_prev = m_scr[...]
        m_next = jnp.maximum(m_prev, jnp.max(m_lane, axis=1, keepdims=True))
        alpha = jnp.exp2(m_prev - m_next)
        p_tiles = [jnp.exp2(t - m_next) for t in tiles]
        l_scr[...] = alpha * l_scr[...] + functools.reduce(jnp.add, p_tiles)
        p = jnp.concatenate([t.astype(BF16) for t in p_tiles], axis=1)
        pv = jnp.dot(p, v_ref[0, pl.ds(off, tk), :], preferred_element_type=F32)
        acc_scr[...] = alpha * acc_scr[...] + pv
        m_scr[...] = m_next
        return carry

    lax.fori_loop(0, s_len // tk, body, 0)
    l = jnp.sum(l_scr[...], axis=1, keepdims=True)
    o_ref[0] = (acc_scr[...] / l).astype(BF16)


def _mla_attn(q, kT, v, *, tq, tk):
    b, s, _ = q.shape
    grid = (b, MLA_HEADS, s // tq)
    return pl.pallas_call(
        functools.partial(_mla_attn_kernel, tk=tk),
        grid=grid,
        in_specs=[
            pl.BlockSpec((1, tq, QK_PAD), lambda bi, h, i: (bi, i, h)),
            pl.BlockSpec((1, QK_PAD, s), lambda bi, h, i: (bi, h, 0)),
            pl.BlockSpec((1, s, V_HEAD), lambda bi, h, i: (bi, 0, h)),
        ],
        out_specs=pl.BlockSpec((1, tq, V_HEAD), lambda bi, h, i: (bi, i, h)),
        out_shape=jax.ShapeDtypeStruct((b, s, MLA_HEADS * V_HEAD), BF16),
        scratch_shapes=[pltpu.VMEM((tq, LANES), F32), pltpu.VMEM((tq, LANES), F32),
                        pltpu.VMEM((tq, V_HEAD), F32)],
        compiler_params=_params(3),
        name="mla_attn",
    )(q, kT, v)


def _na_proj_kernel(x_ref, g_ref, w_ref, o_ref, *, q_scale):
    h = _rms(x_ref[0], g_ref[...]).astype(BF16)
    qkv = jnp.dot(h, w_ref[...], preferred_element_type=F32)
    o_ref[0, :, :D_MODEL] = (qkv[:, :D_MODEL] * q_scale).astype(BF16)
    o_ref[0, :, D_MODEL:] = qkv[:, D_MODEL:].astype(BF16)


def _na_proj(x, g, w_qkv, *, tm):
    b, s, _ = x.shape
    row = lambda bi, i: (bi, i, 0)
    return pl.pallas_call(
        functools.partial(_na_proj_kernel, q_scale=NA_HEAD_DIM ** -0.5),
        grid=(b, s // tm),
        in_specs=[pl.BlockSpec((1, tm, D_MODEL), row), _const_spec((1, D_MODEL)),
                  _const_spec((D_MODEL, 3 * D_MODEL))],
        out_specs=pl.BlockSpec((1, tm, 3 * D_MODEL), row),
        out_shape=jax.ShapeDtypeStruct((b, s, 3 * D_MODEL), BF16),
        compiler_params=_params(2),
        name="na_proj",
    )(x, g.reshape(1, -1), w_qkv.astype(BF16))


def _na_bias_table(rpb):
    pat = np.arange(NA_KH)[:, None, None, None]
    c = np.arange(GRID_W)[None, :, None, None]
    j = np.arange(NA_KH)[None, None, :, None]
    kc = np.arange(GRID_W)[None, None, None, :]
    c0 = np.clip(c - NA_KW // 2, 0, GRID_W - NA_KW)
    valid = np.broadcast_to((kc >= c0) & (kc < c0 + NA_KW), (NA_KH, GRID_W, NA_KH, GRID_W))
    rel_row = np.broadcast_to(j - pat + NA_KH - 1, valid.shape)
    rel_col = np.broadcast_to(np.clip(kc - c + NA_KW - 1, 0, 2 * NA_KW - 2), valid.shape)
    tbl = rpb[:, rel_row, rel_col]
    tbl = jnp.where(valid[None], tbl, MASK_VALUE).astype(F32)
    tbl = tbl.reshape(NA_PAIRS, 2, NA_KH, GRID_W, NA_WIN)
    return tbl.transpose(2, 0, 1, 3, 4).reshape(NA_KH, NA_PAIRS, 2 * GRID_W, NA_WIN)


def _na_attn_kernel(q_ref, k_ref, v_ref, bias_ref, o_ref, *, rows_per_step, n_rows):
    rb = pl.program_id(2)
    lane = lax.broadcasted_iota(jnp.int32, (GRID_W, LANES), 1)
    first = lane < NA_HEAD_DIM

    def body(i, carry):
        r = rb * rows_per_step + i
        r0 = jnp.clip(r - NA_KH // 2, 0, n_rows - NA_KH)
        q2 = q_ref[0, pl.ds(pl.multiple_of(i * GRID_W, GRID_W), GRID_W), :]
        zero = jnp.zeros_like(q2)
        qs = jnp.concatenate([jnp.where(first, q2, zero), jnp.where(first, zero, q2)], axis=0)
        koff = pl.multiple_of(r0 * GRID_W, GRID_W)
        kw = k_ref[0, pl.ds(koff, NA_WIN), :]
        vw = v_ref[0, pl.ds(koff, NA_WIN), :]
        s = lax.dot_general(qs, kw, (((1,), (1,)), ((), ())), preferred_element_type=F32)
        s = s + bias_ref[r - r0, 0]
        m = jnp.max(s, axis=1, keepdims=True)
        p = jnp.exp(s - m)
        l = jnp.sum(p, axis=1, keepdims=True)
        pv = jnp.dot(p.astype(BF16), vw, preferred_element_type=F32) / l
        o = jnp.where(first, pv[:GRID_W], pv[GRID_W:])
        o_ref[0, pl.ds(pl.multiple_of(i * GRID_W, GRID_W), GRID_W), :] = o.astype(BF16)
        return carry

    lax.fori_loop(0, rows_per_step, body, 0)


def _na_attn(qkv, bias, *, rows_per_step):
    b, s, _ = qkv.shape
    n_rows = s // GRID_W
    tq = rows_per_step * GRID_W
    grid = (NA_PAIRS, b, n_rows // rows_per_step)
    return pl.pallas_call(
        functools.partial(_na_attn_kernel, rows_per_step=rows_per_step, n_rows=n_rows),
        grid=grid,
        in_specs=[
            pl.BlockSpec((1, tq, LANES), lambda p, bi, i: (bi, i, p)),
            pl.BlockSpec((1, s, LANES), lambda p, bi, i: (bi, 0, NA_PAIRS + p)),
            pl.BlockSpec((1, s, LANES), lambda p, bi, i: (bi, 0, 2 * NA_PAIRS + p)),
            pl.BlockSpec((NA_KH, 1, 2 * GRID_W, NA_WIN), lambda p, bi, i: (0, p, 0, 0)),
        ],
        out_specs=pl.BlockSpec((1, tq, LANES), lambda p, bi, i: (bi, i, p)),
        out_shape=jax.ShapeDtypeStruct((b, s, D_MODEL), BF16),
        compiler_params=_params(3),
        name="na_attn",
    )(qkv, qkv, qkv, bias)


def _post_kernel(o_ref, x_ref, wo_ref, g_ref, w1_ref, w2_ref, gf_ref, y_ref, *, ff_chunk, final):
    x = x_ref[0] + jnp.dot(o_ref[0], wo_ref[...], preferred_element_type=F32)
    h = _rms(x, g_ref[...]).astype(BF16)
    y = x
    for c in range(D_FF // ff_chunk):
        a = jnp.dot(h, w1_ref[:, c * ff_chunk:(c + 1) * ff_chunk], preferred_element_type=F32)
        a = jnp.maximum(a, 0.0)
        y = y + jnp.dot((a * a).astype(BF16), w2_ref[c * ff_chunk:(c + 1) * ff_chunk, :],
                        preferred_element_type=F32)
    if final:
        y = _rms(y, gf_ref[...])
    y_ref[0] = y


def _post(o, x, w_o, g_mlp, w1, w2, g_final, *, tm, final):
    b, s, d_o = o.shape
    row = lambda bi, i: (bi, i, 0)
    return pl.pallas_call(
        functools.partial(_post_kernel, ff_chunk=1024, final=final),
        grid=(b, s // tm),
        in_specs=[
            pl.BlockSpec((1, tm, d_o), row),
            pl.BlockSpec((1, tm, D_MODEL), row),
            _const_spec((d_o, D_MODEL)),
            _const_spec((1, D_MODEL)),
            _const_spec((D_MODEL, D_FF)),
            _const_spec((D_FF, D_MODEL)),
            _const_spec((1, D_MODEL)),
        ],
        out_specs=pl.BlockSpec((1, tm, D_MODEL), row),
        out_shape=jax.ShapeDtypeStruct((b, s, D_MODEL), F32),
        compiler_params=_params(2),
        name="post_final" if final else "post",
    )(o, x, w_o.astype(BF16), g_mlp.reshape(1, -1), w1.astype(BF16), w2.astype(BF16),
      g_final.reshape(1, -1))


def _trunk(x, attn_norm, mlp_norm, final_norm, mla_w_dq, mla_q_norm, mla_w_uq, mla_w_dkv,
           mla_kv_norm, mla_w_ukv, mla_w_o, na_w_qkv, na_bias, na_w_o, mlp_w1, mlp_w2):
    s = x.shape[1]
    tm = min(512, s)
    q, kT, v = _mla_proj(x, attn_norm[0], mla_w_dq[0], mla_q_norm[0], mla_w_uq[0], mla_w_dkv[0],
                         mla_kv_norm[0], mla_w_ukv[0], tm=min(256, s))
    o = _mla_attn(q, kT, v, tq=min(512, s), tk=min(512, s))
    x = _post(o, x, mla_w_o[0], mlp_norm[0], mlp_w1[0], mlp_w2[0], final_norm, tm=tm, final=False)
    qkv = _na_proj(x, attn_norm[1], na_w_qkv[0], tm=tm)
    o = _na_attn(qkv, na_bias, rows_per_step=min(8, s // GRID_W))
    return _post(o, x, na_w_o[0], mlp_norm[1], mlp_w1[1], mlp_w2[1], final_norm, tm=tm, final=True)


def kernel(x_prompt, x_sample, attn_norm, mlp_norm, final_norm, mla_w_dq, mla_q_norm, mla_w_uq, mla_w_dkv, mla_kv_norm, mla_w_ukv, mla_w_o, na_w_qkv, na_rpb, na_w_o, mlp_w1, mlp_w2):
    na_bias = _na_bias_table(na_rpb[0])
    args = (attn_norm, mlp_norm, final_norm, mla_w_dq, mla_q_norm, mla_w_uq, mla_w_dkv, mla_kv_norm,
            mla_w_ukv, mla_w_o, na_w_qkv, na_bias, na_w_o, mlp_w1, mlp_w2)
    return (_trunk(x_prompt, *args), _trunk(x_sample, *args))
```

```python
import functools
import math

import numpy as np
import jax
import jax.numpy as jnp
from jax import lax
from jax.experimental import pallas as pl
from jax.experimental.pallas import tpu as pltpu

D_MODEL = 1024
GRID_W = 64
MLA_HEADS = 16
Q_LORA = 384
KV_LORA = 256
QK_NOPE = 128
QK_ROPE = 64
V_HEAD = 128
ROPE_THETA = 10000.0
NA_HEADS = 16
NA_HEAD_DIM = D_MODEL // NA_HEADS
NA_KH = 8
NA_KW = 16
D_FF = 4 * D_MODEL
EPS = 1e-6

LANES = 128
QK_PAD = 2 * LANES
NA_PAIRS = NA_HEADS * NA_HEAD_DIM // LANES
NA_WIN = NA_KH * GRID_W
MASK_VALUE = -1e30
VMEM_LIMIT = 56 * 1024 * 1024

F32 = jnp.float32
BF16 = jnp.bfloat16


def _rms(x, g):
    return x * lax.rsqrt(jnp.mean(x * x, axis=-1, keepdims=True) + EPS) * g


def _const_spec(shape):
    zeros = (0,) * len(shape)
    return pl.BlockSpec(shape, lambda *_: zeros, pipeline_mode=pl.Buffered(1))


def _params(n_axes):
    return pltpu.CompilerParams(
        dimension_semantics=("arbitrary",) * n_axes, vmem_limit_bytes=VMEM_LIMIT)


def _mla_proj_kernel(x_ref, g_ref, wdq_ref, qn_ref, wuq_ref, wdkvc_ref, wkr_ref, kvn_ref,
                     wukT_ref, wuv_ref, tq0_ref, tq1_ref, tk0_ref, tk1_ref,
                     q_ref, kT_ref, v_ref, *, q_scale):
    h = _rms(x_ref[0], g_ref[...]).astype(BF16)
    cq = jnp.dot(h, wdq_ref[...], preferred_element_type=F32)
    cq = _rms(cq, qn_ref[...]).astype(BF16)
    q = jnp.dot(cq, wuq_ref[...], preferred_element_type=F32)
    tq0 = tq0_ref[...]
    tq1 = tq1_ref[...]
    for hd in range(MLA_HEADS):
        base = hd * QK_PAD
        nope = q[:, base:base + LANES] * q_scale
        r = q[:, base + LANES:base + QK_PAD]
        roped = r * tq0 + pltpu.roll(r, LANES // 2, axis=1) * tq1
        q_ref[0, :, base:base + LANES] = nope.astype(BF16)
        q_ref[0, :, base + LANES:base + QK_PAD] = roped.astype(BF16)

    ckv = jnp.dot(h, wdkvc_ref[...], preferred_element_type=F32)
    ckv = _rms(ckv, kvn_ref[...]).astype(BF16)
    kr = jnp.dot(h, wkr_ref[...], preferred_element_type=F32)
    kr = kr * tk0_ref[...] + pltpu.roll(kr, LANES // 2, axis=1) * tk1_ref[...]
    krT = kr.T[:QK_ROPE].astype(BF16)
    knT = lax.dot_general(wukT_ref[...], ckv, (((1,), (1,)), ((), ())),
                          preferred_element_type=F32).astype(BF16)
    zpad = jnp.zeros((QK_PAD - QK_NOPE - QK_ROPE, krT.shape[1]), BF16)
    for hd in range(MLA_HEADS):
        base = hd * QK_PAD
        kT_ref[0, base:base + QK_NOPE, :] = knT[hd * QK_NOPE:(hd + 1) * QK_NOPE]
        kT_ref[0, base + QK_NOPE:base + QK_NOPE + QK_ROPE, :] = krT
        kT_ref[0, base + QK_NOPE + QK_ROPE:base + QK_PAD, :] = zpad
    v = jnp.dot(ckv, wuv_ref[...], preferred_element_type=F32).astype(BF16)
    ones = jnp.ones((v.shape[0], V_HEAD), BF16)
    for hd in range(MLA_HEADS):
        v_ref[0, :, 2 * hd * V_HEAD:(2 * hd + 1) * V_HEAD] = v[:, hd * V_HEAD:(hd + 1) * V_HEAD]
        v_ref[0, :, (2 * hd + 1) * V_HEAD:(2 * hd + 2) * V_HEAD] = ones


def _mla_proj(x, g, w_dq, q_norm, w_uq, w_dkv, kv_norm, w_ukv, *, tm):
    b, s, _ = x.shape
    q_scale = (QK_NOPE + QK_ROPE) ** -0.5 * math.log2(math.e)

    half = QK_ROPE // 2
    wq = w_uq.reshape(Q_LORA, MLA_HEADS, QK_NOPE + QK_ROPE)
    x1, x2 = wq[..., QK_NOPE:QK_NOPE + half], wq[..., QK_NOPE + half:]
    wuq_pad = jnp.concatenate([wq[..., :QK_NOPE], x1, x2, x2, x1], axis=-1)
    wuq_pad = wuq_pad.reshape(Q_LORA, MLA_HEADS * QK_PAD).astype(BF16)
    wdkv_c = w_dkv[:, :KV_LORA].astype(BF16)
    k1, k2 = w_dkv[:, KV_LORA:KV_LORA + half], w_dkv[:, KV_LORA + half:]
    w_kr = jnp.concatenate([k1, k2, k2, k1], axis=-1).astype(BF16)
    wkv = w_ukv.reshape(KV_LORA, MLA_HEADS, QK_NOPE + V_HEAD)
    w_ukT = wkv[..., :QK_NOPE].reshape(KV_LORA, MLA_HEADS * QK_NOPE).T.astype(BF16)
    w_uv = wkv[..., QK_NOPE:].reshape(KV_LORA, MLA_HEADS * V_HEAD).astype(BF16)

    inv = ROPE_THETA ** (-jnp.arange(0, QK_ROPE, 2, dtype=F32) / QK_ROPE)
    ang = jnp.arange(s, dtype=F32)[:, None] * inv[None, :]
    cos, sin = jnp.cos(ang), jnp.sin(ang)
    zero = jnp.zeros_like(cos)
    t0 = jnp.concatenate([cos, cos, zero, zero], axis=-1)
    t1 = jnp.concatenate([-sin, sin, zero, zero], axis=-1)

    grid = (b, s // tm)
    row = lambda bi, i: (bi, i, 0)
    tab = pl.BlockSpec((tm, LANES), lambda bi, i: (i, 0))
    return pl.pallas_call(
        functools.partial(_mla_proj_kernel, q_scale=q_scale),
        grid=grid,
        in_specs=[
            pl.BlockSpec((1, tm, D_MODEL), row),
            _const_spec((1, D_MODEL)),
            _const_spec((D_MODEL, Q_LORA)),
            _const_spec((1, Q_LORA)),
            _const_spec((Q_LORA, MLA_HEADS * QK_PAD)),
            _const_spec((D_MODEL, KV_LORA)),
            _const_spec((D_MODEL, LANES)),
            _const_spec((1, KV_LORA)),
            _const_spec((MLA_HEADS * QK_NOPE, KV_LORA)),
            _const_spec((KV_LORA, MLA_HEADS * V_HEAD)),
            tab, tab, tab, tab,
        ],
        out_specs=[
            pl.BlockSpec((1, tm, MLA_HEADS * QK_PAD), row),
            pl.BlockSpec((1, MLA_HEADS * QK_PAD, tm), lambda bi, i: (bi, 0, i)),
            pl.BlockSpec((1, tm, 2 * MLA_HEADS * V_HEAD), row),
        ],
        out_shape=[
            jax.ShapeDtypeStruct((b, s, MLA_HEADS * QK_PAD), BF16),
            jax.ShapeDtypeStruct((b, MLA_HEADS * QK_PAD, s), BF16),
            jax.ShapeDtypeStruct((b, s, 2 * MLA_HEADS * V_HEAD), BF16),
        ],
        compiler_params=_params(2),
        name="mla_proj",
    )(x, g.reshape(1, -1), w_dq.astype(BF16), q_norm.reshape(1, -1), wuq_pad, wdkv_c, w_kr,
      kv_norm.reshape(1, -1), w_ukT, w_uv, t0 * q_scale, t1 * q_scale, t0, t1)


def _mla_attn_kernel(q_ref, kT_ref, v_ref, o_ref, sa_scr, sb_scr, m_scr, acc_scr, *, tk):
    nk = v_ref.shape[1] // tk
    q = q_ref[0]
    m_scr[...] = jnp.full(m_scr.shape, MASK_VALUE, F32)
    acc_scr[...] = jnp.zeros(acc_scr.shape, F32)

    def scores(j, dst):
        off = pl.multiple_of(j * tk, tk)
        dst[...] = jnp.dot(q, kT_ref[0, :, pl.ds(off, tk)], preferred_element_type=F32)

    def softmax_pv(src, j):
        off = pl.multiple_of(j * tk, tk)
        cols = [slice(c * LANES, (c + 1) * LANES) for c in range(tk // LANES)]
        m_lane = functools.reduce(jnp.maximum, [src[:, c] for c in cols])
        m_prev = m_scr[...]
        m_next = jnp.maximum(m_prev, jnp.max(m_lane, axis=1, keepdims=True))
        alpha = jnp.exp2(m_prev - m_next)
        p = jnp.concatenate([jnp.exp2(src[:, c] - m_next).astype(BF16) for c in cols], axis=1)
        pv = jnp.dot(p, v_ref[0, pl.ds(off, tk), :], preferred_element_type=F32)
        acc_scr[:, :V_HEAD] = alpha * acc_scr[:, :V_HEAD] + pv[:, :V_HEAD]
        acc_scr[:, V_HEAD:] = alpha * acc_scr[:, V_HEAD:] + pv[:, V_HEAD:]
        m_scr[...] = m_next

    scores(0, sa_scr)

    def body(t, carry):
        j = 2 * t
        scores(j + 1, sb_scr)
        softmax_pv(sa_scr, j)
        scores(j + 2, sa_scr)
        softmax_pv(sb_scr, j + 1)
        return carry

    if nk > 2:
        lax.fori_loop(0, nk // 2 - 1, body, 0)
    scores(nk - 1, sb_scr)
    softmax_pv(sa_scr, nk - 2)
    softmax_pv(sb_scr, nk - 1)
    o_ref[0] = (acc_scr[:, :V_HEAD] / acc_scr[:, V_HEAD:]).astype(BF16)


def _mla_attn(q, kT, v, *, tq, tk):
    b, s, _ = q.shape
    assert s % (2 * tk) == 0 and s % tq == 0
    grid = (b, MLA_HEADS, s // tq)
    return pl.pallas_call(
        functools.partial(_mla_attn_kernel, tk=tk),
        grid=grid,
        in_specs=[
            pl.BlockSpec((1, tq, QK_PAD), lambda bi, h, i: (bi, i, h)),
            pl.BlockSpec((1, QK_PAD, s), lambda bi, h, i: (bi, h, 0)),
            pl.BlockSpec((1, s, 2 * V_HEAD), lambda bi, h, i: (bi, 0, h)),
        ],
        out_specs=pl.BlockSpec((1, tq, V_HEAD), lambda bi, h, i: (bi, i, h)),
        out_shape=jax.ShapeDtypeStruct((b, s, MLA_HEADS * V_HEAD), BF16),
        scratch_shapes=[pltpu.VMEM((tq, tk), F32), pltpu.VMEM((tq, tk), F32),
                        pltpu.VMEM((tq, LANES), F32), pltpu.VMEM((tq, 2 * V_HEAD), F32)],
        compiler_params=_params(3),
        name="mla_attn",
    )(q, kT, v)


def _na_proj_kernel(x_ref, g_ref, w_ref, o_ref, *, q_scale):
    h = _rms(x_ref[0], g_ref[...]).astype(BF16)
    qkv = jnp.dot(h, w_ref[...], preferred_element_type=F32)
    o_ref[0, :, :D_MODEL] = (qkv[:, :D_MODEL] * q_scale).astype(BF16)
    o_ref[0, :, D_MODEL:] = qkv[:, D_MODEL:].astype(BF16)


def _na_proj(x, g, w_qkv, *, tm):
    b, s, _ = x.shape
    row = lambda bi, i: (bi, i, 0)
    return pl.pallas_call(
        functools.partial(_na_proj_kernel, q_scale=NA_HEAD_DIM ** -0.5),
        grid=(b, s // tm),
        in_specs=[pl.BlockSpec((1, tm, D_MODEL), row), _const_spec((1, D_MODEL)),
                  _const_spec((D_MODEL, 3 * D_MODEL))],
        out_specs=pl.BlockSpec((1, tm, 3 * D_MODEL), row),
        out_shape=jax.ShapeDtypeStruct((b, s, 3 * D_MODEL), BF16),
        compiler_params=_params(2),
        name="na_proj",
    )(x, g.reshape(1, -1), w_qkv.astype(BF16))


def _na_bias_table(rpb):
    c = np.arange(GRID_W)[:, None]
    kc = np.arange(GRID_W)[None, :]
    c0 = np.clip(c - NA_KW // 2, 0, GRID_W - NA_KW)
    valid = (kc >= c0) & (kc < c0 + NA_KW)
    rel_col = kc - c + NA_KW - 1
    n_rel = 2 * NA_KW - 1
    onehot = (valid[..., None] & (rel_col[..., None] == np.arange(n_rel))).astype(np.float32)
    band = jnp.einsum("hrm,ckm->hrck", rpb.astype(F32), jnp.asarray(onehot),
                      precision=lax.Precision.HIGHEST)
    band = jnp.where(jnp.asarray(valid), band, MASK_VALUE)
    tbl = jnp.stack([band[:, NA_KH - 1 - pat:2 * NA_KH - 1 - pat] for pat in range(NA_KH)])
    tbl = tbl.transpose(0, 1, 3, 2, 4)
    return tbl.reshape(NA_KH, NA_PAIRS, 2 * GRID_W, NA_WIN)


def _na_attn_kernel(q_ref, k_ref, v_ref, bias_ref, o_ref, *, rows_per_step, n_rows):
    rb = pl.program_id(2)
    lane = lax.broadcasted_iota(jnp.int32, (GRID_W, LANES), 1)
    first = lane < NA_HEAD_DIM

    def body(i, carry):
        r = rb * rows_per_step + i
        r0 = jnp.clip(r - NA_KH // 2, 0, n_rows - NA_KH)
        q2 = q_ref[0, pl.ds(pl.multiple_of(i * GRID_W, GRID_W), GRID_W), :]
        zero = jnp.zeros_like(q2)
        qs = jnp.concatenate([jnp.where(first, q2, zero), jnp.where(first, zero, q2)], axis=0)
        koff = pl.multiple_of(r0 * GRID_W, GRID_W)
        kw = k_ref[0, pl.ds(koff, NA_WIN), :]
        vw = v_ref[0, pl.ds(koff, NA_WIN), :]
        s = lax.dot_general(qs, kw, (((1,), (1,)), ((), ())), preferred_element_type=F32)
        s = s + bias_ref[r - r0, 0]
        m = jnp.max(s, axis=1, keepdims=True)
        p = jnp.exp(s - m)
        l = jnp.sum(p, axis=1, keepdims=True)
        pv = jnp.dot(p.astype(BF16), vw, preferred_element_type=F32) / l
        o = jnp.where(first, pv[:GRID_W], pv[GRID_W:])
        o_ref[0, pl.ds(pl.multiple_of(i * GRID_W, GRID_W), GRID_W), :] = o.astype(BF16)
        return carry

    lax.fori_loop(0, rows_per_step, body, 0)


def _na_attn(qkv, bias, *, rows_per_step):
    b, s, _ = qkv.shape
    n_rows = s // GRID_W
    tq = rows_per_step * GRID_W
    grid = (NA_PAIRS, b, n_rows // rows_per_step)
    return pl.pallas_call(
        functools.partial(_na_attn_kernel, rows_per_step=rows_per_step, n_rows=n_rows),
        grid=grid,
        in_specs=[
            pl.BlockSpec((1, tq, LANES), lambda p, bi, i: (bi, i, p)),
            pl.BlockSpec((1, s, LANES), lambda p, bi, i: (bi, 0, NA_PAIRS + p)),
            pl.BlockSpec((1, s, LANES), lambda p, bi, i: (bi, 0, 2 * NA_PAIRS + p)),
            pl.BlockSpec((NA_KH, 1, 2 * GRID_W, NA_WIN), lambda p, bi, i: (0, p, 0, 0)),
        ],
        out_specs=pl.BlockSpec((1, tq, LANES), lambda p, bi, i: (bi, i, p)),
        out_shape=jax.ShapeDtypeStruct((b, s, D_MODEL), BF16),
        compiler_params=_params(3),
        name="na_attn",
    )(qkv, qkv, qkv, bias)


def _post_kernel(o_ref, x_ref, wo_ref, g_ref, w1_ref, w2_ref, gf_ref, y_ref, *, ff_chunk, final):
    x = x_ref[0] + jnp.dot(o_ref[0], wo_ref[...], preferred_element_type=F32)
    h = _rms(x, g_ref[...]).astype(BF16)
    y = x
    for c in range(D_FF // ff_chunk):
        a = jnp.dot(h, w1_ref[:, c * ff_chunk:(c + 1) * ff_chunk], preferred_element_type=F32)
        a = jnp.maximum(a, 0.0)
        y = y + jnp.dot((a * a).astype(BF16), w2_ref[c * ff_chunk:(c + 1) * ff_chunk, :],
                        preferred_element_type=F32)
    if final:
        y = _rms(y, gf_ref[...])
    y_ref[0] = y


def _post(o, x, w_o, g_mlp, w1, w2, g_final, *, tm, final):
    b, s, d_o = o.shape
    row = lambda bi, i: (bi, i, 0)
    return pl.pallas_call(
        functools.partial(_post_kernel, ff_chunk=1024, final=final),
        grid=(b, s // tm),
        in_specs=[
            pl.BlockSpec((1, tm, d_o), row),
            pl.BlockSpec((1, tm, D_MODEL), row),
            _const_spec((d_o, D_MODEL)),
            _const_spec((1, D_MODEL)),
            _const_spec((D_MODEL, D_FF)),
            _const_spec((D_FF, D_MODEL)),
            _const_spec((1, D_MODEL)),
        ],
        out_specs=pl.BlockSpec((1, tm, D_MODEL), row),
        out_shape=jax.ShapeDtypeStruct((b, s, D_MODEL), F32),
        compiler_params=_params(2),
        name="post_final" if final else "post",
    )(o, x, w_o.astype(BF16), g_mlp.reshape(1, -1), w1.astype(BF16), w2.astype(BF16),
      g_final.reshape(1, -1))


def _trunk(x, attn_norm, mlp_norm, final_norm, mla_w_dq, mla_q_norm, mla_w_uq, mla_w_dkv,
           mla_kv_norm, mla_w_ukv, mla_w_o, na_w_qkv, na_bias, na_w_o, mlp_w1, mlp_w2):
    s = x.shape[1]
    tm = min(512, s)
    q, kT, v = _mla_proj(x, attn_norm[0], mla_w_dq[0], mla_q_norm[0], mla_w_uq[0], mla_w_dkv[0],
                         mla_kv_norm[0], mla_w_ukv[0], tm=min(256, s))
    o = _mla_attn(q, kT, v, tq=min(512, s), tk=min(1024, s // 2))
    x = _post(o, x, mla_w_o[0], mlp_norm[0], mlp_w1[0], mlp_w2[0], final_norm, tm=tm, final=False)
    qkv = _na_proj(x, attn_norm[1], na_w_qkv[0], tm=tm)
    o = _na_attn(qkv, na_bias, rows_per_step=min(8, s // GRID_W))
    return _post(o, x, na_w_o[0], mlp_norm[1], mlp_w1[1], mlp_w2[1], final_norm, tm=tm, final=True)


def kernel(x_prompt, x_sample, attn_norm, mlp_norm, final_norm, mla_w_dq, mla_q_norm, mla_w_uq, mla_w_dkv, mla_kv_norm, mla_w_ukv, mla_w_o, na_w_qkv, na_rpb, na_w_o, mlp_w1, mlp_w2):
    na_bias = _na_bias_table(na_rpb[0])
    args = (attn_norm, mlp_norm, final_norm, mla_w_dq, mla_q_norm, mla_w_uq, mla_w_dkv, mla_kv_norm,
            mla_w_ukv, mla_w_o, na_w_qkv, na_bias, na_w_o, mlp_w1, mlp_w2)
    return (_trunk(x_prompt, *args), _trunk(x_sample, *args))
```

```python
import functools
import math

import numpy as np
import jax
import jax.numpy as jnp
from jax import lax
from jax.experimental import pallas as pl
from jax.experimental.pallas import tpu as pltpu

D_MODEL = 1024
GRID_W = 64
MLA_HEADS = 16
Q_LORA = 384
KV_LORA = 256
QK_NOPE = 128
QK_ROPE = 64
V_HEAD = 128
ROPE_THETA = 10000.0
NA_HEADS = 16
NA_HEAD_DIM = D_MODEL // NA_HEADS
NA_KH = 8
NA_KW = 16
D_FF = 4 * D_MODEL
EPS = 1e-6

LANES = 128
QK_PAD = 2 * LANES
NA_PAIRS = NA_HEADS * NA_HEAD_DIM // LANES
NA_WIN = NA_KH * GRID_W
MASK_VALUE = -1e30
VMEM_LIMIT = 56 * 1024 * 1024

F32 = jnp.float32
BF16 = jnp.bfloat16


def _rms(x, g):
    return x * lax.rsqrt(jnp.mean(x * x, axis=-1, keepdims=True) + EPS) * g


def _const_spec(shape):
    zeros = (0,) * len(shape)
    return pl.BlockSpec(shape, lambda *_: zeros, pipeline_mode=pl.Buffered(1))


def _params(n_axes):
    return pltpu.CompilerParams(
        dimension_semantics=("arbitrary",) * n_axes, vmem_limit_bytes=VMEM_LIMIT)


def _mla_proj_kernel(x_ref, g_ref, wdq_ref, qn_ref, wuq_ref, wdkvc_ref, wkr_ref, kvn_ref,
                     wukT_ref, wuv_ref, tq0_ref, tq1_ref, tk0_ref, tk1_ref,
                     q_ref, kT_ref, v_ref, *, q_scale):
    h = _rms(x_ref[0], g_ref[...]).astype(BF16)
    cq = jnp.dot(h, wdq_ref[...], preferred_element_type=F32)
    cq = _rms(cq, qn_ref[...]).astype(BF16)
    q = jnp.dot(cq, wuq_ref[...], preferred_element_type=F32)
    tq0 = tq0_ref[...]
    tq1 = tq1_ref[...]
    for hd in range(MLA_HEADS):
        base = hd * QK_PAD
        nope = q[:, base:base + LANES] * q_scale
        r = q[:, base + LANES:base + QK_PAD]
        roped = r * tq0 + pltpu.roll(r, LANES // 2, axis=1) * tq1
        q_ref[0, :, base:base + LANES] = nope.astype(BF16)
        q_ref[0, :, base + LANES:base + QK_PAD] = roped.astype(BF16)

    ckv = jnp.dot(h, wdkvc_ref[...], preferred_element_type=F32)
    ckv = _rms(ckv, kvn_ref[...]).astype(BF16)
    kr = jnp.dot(h, wkr_ref[...], preferred_element_type=F32)
    kr = kr * tk0_ref[...] + pltpu.roll(kr, LANES // 2, axis=1) * tk1_ref[...]
    krT = kr.T[:QK_ROPE].astype(BF16)
    knT = lax.dot_general(wukT_ref[...], ckv, (((1,), (1,)), ((), ())),
                          preferred_element_type=F32).astype(BF16)
    zpad = jnp.zeros((QK_PAD - QK_NOPE - QK_ROPE, krT.shape[1]), BF16)
    for hd in range(MLA_HEADS):
        base = hd * QK_PAD
        kT_ref[0, base:base + QK_NOPE, :] = knT[hd * QK_NOPE:(hd + 1) * QK_NOPE]
        kT_ref[0, base + QK_NOPE:base + QK_NOPE + QK_ROPE, :] = krT
        kT_ref[0, base + QK_NOPE + QK_ROPE:base + QK_PAD, :] = zpad
    v = jnp.dot(ckv, wuv_ref[...], preferred_element_type=F32).astype(BF16)
    ones = jnp.ones((v.shape[0], V_HEAD), BF16)
    for hd in range(MLA_HEADS):
        v_ref[0, :, 2 * hd * V_HEAD:(2 * hd + 1) * V_HEAD] = v[:, hd * V_HEAD:(hd + 1) * V_HEAD]
        v_ref[0, :, (2 * hd + 1) * V_HEAD:(2 * hd + 2) * V_HEAD] = ones


def _mla_proj(x, g, w_dq, q_norm, w_uq, w_dkv, kv_norm, w_ukv, *, tm):
    b, s, _ = x.shape
    q_scale = (QK_NOPE + QK_ROPE) ** -0.5 * math.log2(math.e)

    half = QK_ROPE // 2
    wq = w_uq.reshape(Q_LORA, MLA_HEADS, QK_NOPE + QK_ROPE)
    x1, x2 = wq[..., QK_NOPE:QK_NOPE + half], wq[..., QK_NOPE + half:]
    wuq_pad = jnp.concatenate([wq[..., :QK_NOPE], x1, x2, x2, x1], axis=-1)
    wuq_pad = wuq_pad.reshape(Q_LORA, MLA_HEADS * QK_PAD).astype(BF16)
    wdkv_c = w_dkv[:, :KV_LORA].astype(BF16)
    k1, k2 = w_dkv[:, KV_LORA:KV_LORA + half], w_dkv[:, KV_LORA + half:]
    w_kr = jnp.concatenate([k1, k2, k2, k1], axis=-1).astype(BF16)
    wkv = w_ukv.reshape(KV_LORA, MLA_HEADS, QK_NOPE + V_HEAD)
    w_ukT = wkv[..., :QK_NOPE].reshape(KV_LORA, MLA_HEADS * QK_NOPE).T.astype(BF16)
    w_uv = wkv[..., QK_NOPE:].reshape(KV_LORA, MLA_HEADS * V_HEAD).astype(BF16)

    inv = ROPE_THETA ** (-jnp.arange(0, QK_ROPE, 2, dtype=F32) / QK_ROPE)
    ang = jnp.arange(s, dtype=F32)[:, None] * inv[None, :]
    cos, sin = jnp.cos(ang), jnp.sin(ang)
    zero = jnp.zeros_like(cos)
    t0 = jnp.concatenate([cos, cos, zero, zero], axis=-1)
    t1 = jnp.concatenate([-sin, sin, zero, zero], axis=-1)

    grid = (b, s // tm)
    row = lambda bi, i: (bi, i, 0)
    tab = pl.BlockSpec((tm, LANES), lambda bi, i: (i, 0))
    return pl.pallas_call(
        functools.partial(_mla_proj_kernel, q_scale=q_scale),
        grid=grid,
        in_specs=[
            pl.BlockSpec((1, tm, D_MODEL), row),
            _const_spec((1, D_MODEL)),
            _const_spec((D_MODEL, Q_LORA)),
            _const_spec((1, Q_LORA)),
            _const_spec((Q_LORA, MLA_HEADS * QK_PAD)),
            _const_spec((D_MODEL, KV_LORA)),
            _const_spec((D_MODEL, LANES)),
            _const_spec((1, KV_LORA)),
            _const_spec((MLA_HEADS * QK_NOPE, KV_LORA)),
            _const_spec((KV_LORA, MLA_HEADS * V_HEAD)),
            tab, tab, tab, tab,
        ],
        out_specs=[
            pl.BlockSpec((1, tm, MLA_HEADS * QK_PAD), row),
            pl.BlockSpec((1, MLA_HEADS * QK_PAD, tm), lambda bi, i: (bi, 0, i)),
            pl.BlockSpec((1, tm, 2 * MLA_HEADS * V_HEAD), row),
        ],
        out_shape=[
            jax.ShapeDtypeStruct((b, s, MLA_HEADS * QK_PAD), BF16),
            jax.ShapeDtypeStruct((b, MLA_HEADS * QK_PAD, s), BF16),
            jax.ShapeDtypeStruct((b, s, 2 * MLA_HEADS * V_HEAD), BF16),
        ],
        compiler_params=_params(2),
        name="mla_proj",
    )(x, g.reshape(1, -1), w_dq.astype(BF16), q_norm.reshape(1, -1), wuq_pad, wdkv_c, w_kr,
      kv_norm.reshape(1, -1), w_ukT, w_uv, t0 * q_scale, t1 * q_scale, t0, t1)


def _mla_attn_kernel(q_ref, kT_ref, v_ref, o_ref, sa_scr, sb_scr, m_scr, acc_scr, *, tk):
    nk = v_ref.shape[1] // tk
    q = q_ref[0]
    m_scr[...] = jnp.full(m_scr.shape, MASK_VALUE, F32)
    acc_scr[...] = jnp.zeros(acc_scr.shape, F32)

    def scores(j, dst):
        off = pl.multiple_of(j * tk, tk)
        dst[...] = jnp.dot(q, kT_ref[0, :, pl.ds(off, tk)], preferred_element_type=F32)

    def softmax_pv(src, j):
        off = pl.multiple_of(j * tk, tk)
        cols = [slice(c * LANES, (c + 1) * LANES) for c in range(tk // LANES)]
        m_lane = functools.reduce(jnp.maximum, [src[:, c] for c in cols])
        m_prev = m_scr[...]
        m_next = jnp.maximum(m_prev, jnp.max(m_lane, axis=1, keepdims=True))
        alpha = jnp.exp2(m_prev - m_next)
        p = jnp.concatenate([jnp.exp2(src[:, c] - m_next).astype(BF16) for c in cols], axis=1)
        pv = jnp.dot(p, v_ref[0, pl.ds(off, tk), :], preferred_element_type=F32)
        acc_scr[:, :V_HEAD] = alpha * acc_scr[:, :V_HEAD] + pv[:, :V_HEAD]
        acc_scr[:, V_HEAD:] = alpha * acc_scr[:, V_HEAD:] + pv[:, V_HEAD:]
        m_scr[...] = m_next

    scores(0, sa_scr)

    def body(t, carry):
        j = 2 * t
        scores(j + 1, sb_scr)
        softmax_pv(sa_scr, j)
        scores(j + 2, sa_scr)
        softmax_pv(sb_scr, j + 1)
        return carry

    if nk > 2:
        lax.fori_loop(0, nk // 2 - 1, body, 0)
    scores(nk - 1, sb_scr)
    softmax_pv(sa_scr, nk - 2)
    softmax_pv(sb_scr, nk - 1)
    o_ref[0] = (acc_scr[:, :V_HEAD] / acc_scr[:, V_HEAD:]).astype(BF16)


def _mla_attn(q, kT, v, *, tq, tk):
    b, s, _ = q.shape
    assert s % (2 * tk) == 0 and s % tq == 0
    grid = (b, MLA_HEADS, s // tq)
    return pl.pallas_call(
        functools.partial(_mla_attn_kernel, tk=tk),
        grid=grid,
        in_specs=[
            pl.BlockSpec((1, tq, QK_PAD), lambda bi, h, i: (bi, i, h)),
            pl.BlockSpec((1, QK_PAD, s), lambda bi, h, i: (bi, h, 0)),
            pl.BlockSpec((1, s, 2 * V_HEAD), lambda bi, h, i: (bi, 0, h)),
        ],
        out_specs=pl.BlockSpec((1, tq, V_HEAD), lambda bi, h, i: (bi, i, h)),
        out_shape=jax.ShapeDtypeStruct((b, s, MLA_HEADS * V_HEAD), BF16),
        scratch_shapes=[pltpu.VMEM((tq, tk), F32), pltpu.VMEM((tq, tk), F32),
                        pltpu.VMEM((tq, LANES), F32), pltpu.VMEM((tq, 2 * V_HEAD), F32)],
        compiler_params=_params(3),
        name="mla_attn",
    )(q, kT, v)


def _na_proj_kernel(x_ref, g_ref, w_ref, o_ref, *, q_scale):
    h = _rms(x_ref[0], g_ref[...]).astype(BF16)
    qkv = jnp.dot(h, w_ref[...], preferred_element_type=F32)
    o_ref[0, :, :D_MODEL] = (qkv[:, :D_MODEL] * q_scale).astype(BF16)
    o_ref[0, :, D_MODEL:] = qkv[:, D_MODEL:].astype(BF16)


def _na_proj(x, g, w_qkv, *, tm):
    b, s, _ = x.shape
    row = lambda bi, i: (bi, i, 0)
    return pl.pallas_call(
        functools.partial(_na_proj_kernel, q_scale=NA_HEAD_DIM ** -0.5),
        grid=(b, s // tm),
        in_specs=[pl.BlockSpec((1, tm, D_MODEL), row), _const_spec((1, D_MODEL)),
                  _const_spec((D_MODEL, 3 * D_MODEL))],
        out_specs=pl.BlockSpec((1, tm, 3 * D_MODEL), row),
        out_shape=jax.ShapeDtypeStruct((b, s, 3 * D_MODEL), BF16),
        compiler_params=_params(2),
        name="na_proj",
    )(x, g.reshape(1, -1), w_qkv.astype(BF16))


def _na_bias_table(rpb):
    c = np.arange(GRID_W)[:, None]
    kc = np.arange(GRID_W)[None, :]
    c0 = np.clip(c - NA_KW // 2, 0, GRID_W - NA_KW)
    valid = (kc >= c0) & (kc < c0 + NA_KW)
    rel_col = kc - c + NA_KW - 1
    n_rel = 2 * NA_KW - 1
    onehot = (valid[..., None] & (rel_col[..., None] == np.arange(n_rel))).astype(np.float32)
    band = jnp.einsum("hrm,ckm->hrck", rpb.astype(F32), jnp.asarray(onehot),
                      precision=lax.Precision.HIGHEST)
    band = jnp.where(jnp.asarray(valid), band, MASK_VALUE)
    tbl = jnp.stack([band[:, NA_KH - 1 - pat:2 * NA_KH - 1 - pat] for pat in range(NA_KH)])
    tbl = tbl.transpose(0, 1, 3, 2, 4)
    return tbl.reshape(NA_KH, NA_PAIRS, 2 * GRID_W, NA_WIN)


def _na_attn_kernel(q_ref, k_ref, v_ref, bias_ref, o_ref, *, rows_per_step, n_rows):
    rb = pl.program_id(2)
    lane = lax.broadcasted_iota(jnp.int32, (GRID_W, LANES), 1)
    first = lane < NA_HEAD_DIM

    def window(i):
        r = rb * rows_per_step + i
        r0 = jnp.clip(r - NA_KH // 2, 0, n_rows - NA_KH)
        return r - r0, pl.multiple_of(r0 * GRID_W, GRID_W)

    def scores(i):
        pat, koff = window(i)
        q2 = q_ref[0, i * GRID_W:(i + 1) * GRID_W, :]
        zero = jnp.zeros_like(q2)
        qs = jnp.concatenate([jnp.where(first, q2, zero), jnp.where(first, zero, q2)], axis=0)
        kw = k_ref[0, pl.ds(koff, NA_WIN), :]
        s = lax.dot_general(qs, kw, (((1,), (1,)), ((), ())), preferred_element_type=F32)
        return s + bias_ref[pat, 0]

    def finish(i, s):
        _, koff = window(i)
        vw = v_ref[0, pl.ds(koff, NA_WIN), :]
        m = jnp.max(s, axis=1, keepdims=True)
        p = jnp.exp(s - m)
        l = jnp.sum(p, axis=1, keepdims=True)
        pv = jnp.dot(p.astype(BF16), vw, preferred_element_type=F32) / l
        o = jnp.where(first, pv[:GRID_W], pv[GRID_W:])
        o_ref[0, i * GRID_W:(i + 1) * GRID_W, :] = o.astype(BF16)

    ahead = min(3, rows_per_step)
    pending = [scores(i) for i in range(ahead)]
    for i in range(rows_per_step):
        if i + ahead < rows_per_step:
            pending.append(scores(i + ahead))
        finish(i, pending[i])
        pending[i] = None


def _na_attn(qkv, bias, *, rows_per_step):
    b, s, _ = qkv.shape
    n_rows = s // GRID_W
    tq = rows_per_step * GRID_W
    grid = (NA_PAIRS, b, n_rows // rows_per_step)
    return pl.pallas_call(
        functools.partial(_na_attn_kernel, rows_per_step=rows_per_step, n_rows=n_rows),
        grid=grid,
        in_specs=[
            pl.BlockSpec((1, tq, LANES), lambda p, bi, i: (bi, i, p)),
            pl.BlockSpec((1, s, LANES), lambda p, bi, i: (bi, 0, NA_PAIRS + p)),
            pl.BlockSpec((1, s, LANES), lambda p, bi, i: (bi, 0, 2 * NA_PAIRS + p)),
            pl.BlockSpec((NA_KH, 1, 2 * GRID_W, NA_WIN), lambda p, bi, i: (0, p, 0, 0)),
        ],
        out_specs=pl.BlockSpec((1, tq, LANES), lambda p, bi, i: (bi, i, p)),
        out_shape=jax.ShapeDtypeStruct((b, s, D_MODEL), BF16),
        compiler_params=_params(3),
        name="na_attn",
    )(qkv, qkv, qkv, bias)


def _post_kernel(o_ref, x_ref, wo_ref, g_ref, w1_ref, w2_ref, gf_ref, y_ref, *, ff_chunk, final):
    x = x_ref[0] + jnp.dot(o_ref[0], wo_ref[...], preferred_element_type=F32)
    h = _rms(x, g_ref[...]).astype(BF16)
    y = x
    for c in range(D_FF // ff_chunk):
        a = jnp.dot(h, w1_ref[:, c * ff_chunk:(c + 1) * ff_chunk], preferred_element_type=F32)
        a = jnp.maximum(a, 0.0)
        y = y + jnp.dot((a * a).astype(BF16), w2_ref[c * ff_chunk:(c + 1) * ff_chunk, :],
                        preferred_element_type=F32)
    if final:
        y = _rms(y, gf_ref[...])
    y_ref[0] = y


def _post(o, x, w_o, g_mlp, w1, w2, g_final, *, tm, final):
    b, s, d_o = o.shape
    row = lambda bi, i: (bi, i, 0)
    return pl.pallas_call(
        functools.partial(_post_kernel, ff_chunk=1024, final=final),
        grid=(b, s // tm),
        in_specs=[
            pl.BlockSpec((1, tm, d_o), row),
            pl.BlockSpec((1, tm, D_MODEL), row),
            _const_spec((d_o, D_MODEL)),
            _const_spec((1, D_MODEL)),
            _const_spec((D_MODEL, D_FF)),
            _const_spec((D_FF, D_MODEL)),
            _const_spec((1, D_MODEL)),
        ],
        out_specs=pl.BlockSpec((1, tm, D_MODEL), row),
        out_shape=jax.ShapeDtypeStruct((b, s, D_MODEL), F32),
        compiler_params=_params(2),
        name="post_final" if final else "post",
    )(o, x, w_o.astype(BF16), g_mlp.reshape(1, -1), w1.astype(BF16), w2.astype(BF16),
      g_final.reshape(1, -1))


def _trunk(x, attn_norm, mlp_norm, final_norm, mla_w_dq, mla_q_norm, mla_w_uq, mla_w_dkv,
           mla_kv_norm, mla_w_ukv, mla_w_o, na_w_qkv, na_bias, na_w_o, mlp_w1, mlp_w2):
    s = x.shape[1]
    tm = min(512, s)
    q, kT, v = _mla_proj(x, attn_norm[0], mla_w_dq[0], mla_q_norm[0], mla_w_uq[0], mla_w_dkv[0],
                         mla_kv_norm[0], mla_w_ukv[0], tm=min(256, s))
    o = _mla_attn(q, kT, v, tq=min(1024, s), tk=min(1024, s // 2))
    x = _post(o, x, mla_w_o[0], mlp_norm[0], mlp_w1[0], mlp_w2[0], final_norm, tm=tm, final=False)
    qkv = _na_proj(x, attn_norm[1], na_w_qkv[0], tm=tm)
    o = _na_attn(qkv, na_bias, rows_per_step=min(16, s // GRID_W))
    return _post(o, x, na_w_o[0], mlp_norm[1], mlp_w1[1], mlp_w2[1], final_norm, tm=tm, final=True)


def kernel(x_prompt, x_sample, attn_norm, mlp_norm, final_norm, mla_w_dq, mla_q_norm, mla_w_uq, mla_w_dkv, mla_kv_norm, mla_w_ukv, mla_w_o, na_w_qkv, na_rpb, na_w_o, mlp_w1, mlp_w2):
    na_bias = _na_bias_table(na_rpb[0])
    args = (attn_norm, mlp_norm, final_norm, mla_w_dq, mla_q_norm, mla_w_uq, mla_w_dkv, mla_kv_norm,
            mla_w_ukv, mla_w_o, na_w_qkv, na_bias, na_w_o, mlp_w1, mlp_w2)
    return (_trunk(x_prompt, *args), _trunk(x_sample, *args))
```

```python
import functools
import math

import numpy as np
import jax
import jax.numpy as jnp
from jax import lax
from jax.experimental import pallas as pl
from jax.experimental.pallas import tpu as pltpu

D_MODEL = 1024
GRID_W = 64
MLA_HEADS = 16
Q_LORA = 384
KV_LORA = 256
QK_NOPE = 128
QK_ROPE = 64
V_HEAD = 128
ROPE_THETA = 10000.0
NA_HEADS = 16
NA_HEAD_DIM = D_MODEL // NA_HEADS
NA_KH = 8
NA_KW = 16
D_FF = 4 * D_MODEL
EPS = 1e-6

LANES = 128
QK_PAD = 2 * LANES
NA_PAIRS = NA_HEADS * NA_HEAD_DIM // LANES
NA_WIN = NA_KH * GRID_W
MASK_VALUE = -1e30
VMEM_LIMIT = 56 * 1024 * 1024

F32 = jnp.float32
BF16 = jnp.bfloat16


def _rms(x, g):
    return x * lax.rsqrt(jnp.mean(x * x, axis=-1, keepdims=True) + EPS) * g


def _const_spec(shape):
    zeros = (0,) * len(shape)
    return pl.BlockSpec(shape, lambda *_: zeros, pipeline_mode=pl.Buffered(1))


def _params(n_axes):
    return pltpu.CompilerParams(
        dimension_semantics=("arbitrary",) * n_axes, vmem_limit_bytes=VMEM_LIMIT)


def _mla_proj_kernel(x_ref, g_ref, wdq_ref, qn_ref, wuq_ref, wdkvc_ref, wkr_ref, kvn_ref,
                     wukT_ref, wuv_ref, tq0_ref, tq1_ref, tk0_ref, tk1_ref,
                     q_ref, kT_ref, v_ref, *, q_scale):
    h = _rms(x_ref[0], g_ref[...]).astype(BF16)
    cq = jnp.dot(h, wdq_ref[...], preferred_element_type=F32)
    cq = _rms(cq, qn_ref[...]).astype(BF16)
    q = jnp.dot(cq, wuq_ref[...], preferred_element_type=F32)
    tq0 = tq0_ref[...]
    tq1 = tq1_ref[...]
    for hd in range(MLA_HEADS):
        base = hd * QK_PAD
        nope = q[:, base:base + LANES] * q_scale
        r = q[:, base + LANES:base + QK_PAD]
        roped = r * tq0 + pltpu.roll(r, LANES // 2, axis=1) * tq1
        q_ref[0, :, base:base + LANES] = nope.astype(BF16)
        q_ref[0, :, base + LANES:base + QK_PAD] = roped.astype(BF16)

    ckv = jnp.dot(h, wdkvc_ref[...], preferred_element_type=F32)
    ckv = _rms(ckv, kvn_ref[...]).astype(BF16)
    kr = jnp.dot(h, wkr_ref[...], preferred_element_type=F32)
    kr = kr * tk0_ref[...] + pltpu.roll(kr, LANES // 2, axis=1) * tk1_ref[...]
    krT = kr.T[:QK_ROPE].astype(BF16)
    knT = lax.dot_general(wukT_ref[...], ckv, (((1,), (1,)), ((), ())),
                          preferred_element_type=F32).astype(BF16)
    zpad = jnp.zeros((QK_PAD - QK_NOPE - QK_ROPE, krT.shape[1]), BF16)
    for hd in range(MLA_HEADS):
        base = hd * QK_PAD
        kT_ref[0, base:base + QK_NOPE, :] = knT[hd * QK_NOPE:(hd + 1) * QK_NOPE]
        kT_ref[0, base + QK_NOPE:base + QK_NOPE + QK_ROPE, :] = krT
        kT_ref[0, base + QK_NOPE + QK_ROPE:base + QK_PAD, :] = zpad
    v = jnp.dot(ckv, wuv_ref[...], preferred_element_type=F32).astype(BF16)
    ones = jnp.ones((v.shape[0], V_HEAD), BF16)
    for hd in range(MLA_HEADS):
        v_ref[0, :, 2 * hd * V_HEAD:(2 * hd + 1) * V_HEAD] = v[:, hd * V_HEAD:(hd + 1) * V_HEAD]
        v_ref[0, :, (2 * hd + 1) * V_HEAD:(2 * hd + 2) * V_HEAD] = ones


def _mla_proj(x, g, w_dq, q_norm, w_uq, w_dkv, kv_norm, w_ukv, *, tm):
    b, s, _ = x.shape
    q_scale = (QK_NOPE + QK_ROPE) ** -0.5 * math.log2(math.e)

    half = QK_ROPE // 2
    wq = w_uq.reshape(Q_LORA, MLA_HEADS, QK_NOPE + QK_ROPE)
    x1, x2 = wq[..., QK_NOPE:QK_NOPE + half], wq[..., QK_NOPE + half:]
    wuq_pad = jnp.concatenate([wq[..., :QK_NOPE], x1, x2, x2, x1], axis=-1)
    wuq_pad = wuq_pad.reshape(Q_LORA, MLA_HEADS * QK_PAD).astype(BF16)
    wdkv_c = w_dkv[:, :KV_LORA].astype(BF16)
    k1, k2 = w_dkv[:, KV_LORA:KV_LORA + half], w_dkv[:, KV_LORA + half:]
    w_kr = jnp.concatenate([k1, k2, k2, k1], axis=-1).astype(BF16)
    wkv = w_ukv.reshape(KV_LORA, MLA_HEADS, QK_NOPE + V_HEAD)
    w_ukT = wkv[..., :QK_NOPE].reshape(KV_LORA, MLA_HEADS * QK_NOPE).T.astype(BF16)
    w_uv = wkv[..., QK_NOPE:].reshape(KV_LORA, MLA_HEADS * V_HEAD).astype(BF16)

    inv = ROPE_THETA ** (-jnp.arange(0, QK_ROPE, 2, dtype=F32) / QK_ROPE)
    ang = jnp.arange(s, dtype=F32)[:, None] * inv[None, :]
    cos, sin = jnp.cos(ang), jnp.sin(ang)
    zero = jnp.zeros_like(cos)
    t0 = jnp.concatenate([cos, cos, zero, zero], axis=-1)
    t1 = jnp.concatenate([-sin, sin, zero, zero], axis=-1)

    grid = (b, s // tm)
    row = lambda bi, i: (bi, i, 0)
    tab = pl.BlockSpec((tm, LANES), lambda bi, i: (i, 0))
    return pl.pallas_call(
        functools.partial(_mla_proj_kernel, q_scale=q_scale),
        grid=grid,
        in_specs=[
            pl.BlockSpec((1, tm, D_MODEL), row),
            _const_spec((1, D_MODEL)),
            _const_spec((D_MODEL, Q_LORA)),
            _const_spec((1, Q_LORA)),
            _const_spec((Q_LORA, MLA_HEADS * QK_PAD)),
            _const_spec((D_MODEL, KV_LORA)),
            _const_spec((D_MODEL, LANES)),
            _const_spec((1, KV_LORA)),
            _const_spec((MLA_HEADS * QK_NOPE, KV_LORA)),
            _const_spec((KV_LORA, MLA_HEADS * V_HEAD)),
            tab, tab, tab, tab,
        ],
        out_specs=[
            pl.BlockSpec((1, tm, MLA_HEADS * QK_PAD), row),
            pl.BlockSpec((1, MLA_HEADS * QK_PAD, tm), lambda bi, i: (bi, 0, i)),
            pl.BlockSpec((1, tm, 2 * MLA_HEADS * V_HEAD), row),
        ],
        out_shape=[
            jax.ShapeDtypeStruct((b, s, MLA_HEADS * QK_PAD), BF16),
            jax.ShapeDtypeStruct((b, MLA_HEADS * QK_PAD, s), BF16),
            jax.ShapeDtypeStruct((b, s, 2 * MLA_HEADS * V_HEAD), BF16),
        ],
        compiler_params=_params(2),
        name="mla_proj",
    )(x, g.reshape(1, -1), w_dq.astype(BF16), q_norm.reshape(1, -1), wuq_pad, wdkv_c, w_kr,
      kv_norm.reshape(1, -1), w_ukT, w_uv, t0 * q_scale, t1 * q_scale, t0, t1)


def _mla_attn_kernel(q_ref, kT_ref, v_ref, o_ref, sa_scr, sb_scr, m_scr, acc_scr, *, tk, chunks_per_trip):
    nk = v_ref.shape[1] // tk
    m_scr[...] = jnp.full(m_scr.shape, MASK_VALUE, F32)
    acc_scr[...] = jnp.zeros(acc_scr.shape, F32)
    cols = [slice(c * LANES, (c + 1) * LANES) for c in range(tk // LANES)]
    s_bufs = (sa_scr, sb_scr)

    def scores(j, dst):
        off = pl.multiple_of(j * tk, tk)
        dst[...] = jnp.dot(q_ref[0], kT_ref[0, :, pl.ds(off, tk)], preferred_element_type=F32)

    def softmax_pv(src, j):
        off = pl.multiple_of(j * tk, tk)
        m_lane = functools.reduce(jnp.maximum, [src[:, c] for c in cols])
        m_prev = m_scr[...]
        m_next = jnp.maximum(m_prev, jnp.max(m_lane, axis=1, keepdims=True))
        alpha = jnp.exp2(m_prev - m_next)
        p = jnp.concatenate([jnp.exp2(src[:, c] - m_next).astype(BF16) for c in cols], axis=1)
        pv = jnp.dot(p, v_ref[0, pl.ds(off, tk), :], preferred_element_type=F32)
        acc_scr[:, :V_HEAD] = alpha * acc_scr[:, :V_HEAD] + pv[:, :V_HEAD]
        acc_scr[:, V_HEAD:] = alpha * acc_scr[:, V_HEAD:] + pv[:, V_HEAD:]
        m_scr[...] = m_next

    def chunk(j, parity, last):
        if not last:
            scores(j + 1, s_bufs[1 - parity])
        softmax_pv(s_bufs[parity], j)

    scores(0, sa_scr)
    n_trips = (nk - 1) // chunks_per_trip

    def body(t, carry):
        for k in range(chunks_per_trip):
            chunk(t * chunks_per_trip + k, k % 2, False)
        return carry

    if n_trips > 0:
        lax.fori_loop(0, n_trips, body, 0)
    for j in range(n_trips * chunks_per_trip, nk):
        chunk(j, j % 2, j == nk - 1)
    o_ref[0] = (acc_scr[:, :V_HEAD] / acc_scr[:, V_HEAD:]).astype(BF16)


def _mla_attn(q, kT, v, *, tq, tk, chunks_per_trip=4):
    b, s, _ = q.shape
    assert s % tk == 0 and s % tq == 0 and chunks_per_trip % 2 == 0
    grid = (b, MLA_HEADS, s // tq)
    return pl.pallas_call(
        functools.partial(_mla_attn_kernel, tk=tk, chunks_per_trip=chunks_per_trip),
        grid=grid,
        in_specs=[
            pl.BlockSpec((1, tq, QK_PAD), lambda bi, h, i: (bi, i, h)),
            pl.BlockSpec((1, QK_PAD, s), lambda bi, h, i: (bi, h, 0)),
            pl.BlockSpec((1, s, 2 * V_HEAD), lambda bi, h, i: (bi, 0, h)),
        ],
        out_specs=pl.BlockSpec((1, tq, V_HEAD), lambda bi, h, i: (bi, i, h)),
        out_shape=jax.ShapeDtypeStruct((b, s, MLA_HEADS * V_HEAD), BF16),
        scratch_shapes=[pltpu.VMEM((tq, tk), F32), pltpu.VMEM((tq, tk), F32),
                        pltpu.VMEM((tq, LANES), F32), pltpu.VMEM((tq, 2 * V_HEAD), F32)],
        compiler_params=_params(3),
        name="mla_attn",
    )(q, kT, v)


def _na_proj_kernel(x_ref, g_ref, w_ref, o_ref, *, q_scale):
    h = _rms(x_ref[0], g_ref[...]).astype(BF16)
    qkv = jnp.dot(h, w_ref[...], preferred_element_type=F32)
    o_ref[0, :, :D_MODEL] = (qkv[:, :D_MODEL] * q_scale).astype(BF16)
    o_ref[0, :, D_MODEL:] = qkv[:, D_MODEL:].astype(BF16)


def _na_proj(x, g, w_qkv, *, tm):
    b, s, _ = x.shape
    row = lambda bi, i: (bi, i, 0)
    return pl.pallas_call(
        functools.partial(_na_proj_kernel, q_scale=NA_HEAD_DIM ** -0.5),
        grid=(b, s // tm),
        in_specs=[pl.BlockSpec((1, tm, D_MODEL), row), _const_spec((1, D_MODEL)),
                  _const_spec((D_MODEL, 3 * D_MODEL))],
        out_specs=pl.BlockSpec((1, tm, 3 * D_MODEL), row),
        out_shape=jax.ShapeDtypeStruct((b, s, 3 * D_MODEL), BF16),
        compiler_params=_params(2),
        name="na_proj",
    )(x, g.reshape(1, -1), w_qkv.astype(BF16))


def _na_bias_table(rpb):
    c = np.arange(GRID_W)[:, None]
    kc = np.arange(GRID_W)[None, :]
    c0 = np.clip(c - NA_KW // 2, 0, GRID_W - NA_KW)
    valid = (kc >= c0) & (kc < c0 + NA_KW)
    rel_col = kc - c + NA_KW - 1
    n_rel = 2 * NA_KW - 1
    onehot = (valid[..., None] & (rel_col[..., None] == np.arange(n_rel))).astype(np.float32)
    band = jnp.einsum("hrm,ckm->hrck", rpb.astype(F32), jnp.asarray(onehot),
                      precision=lax.Precision.HIGHEST)
    band = jnp.where(jnp.asarray(valid), band, MASK_VALUE)
    tbl = jnp.stack([band[:, NA_KH - 1 - pat:2 * NA_KH - 1 - pat] for pat in range(NA_KH)])
    tbl = tbl.transpose(0, 1, 3, 2, 4)
    return tbl.reshape(NA_KH, NA_PAIRS, 2 * GRID_W, NA_WIN)


def _na_attn_kernel(q_ref, k_ref, v_ref, bias_ref, o_ref, *, rows_per_step, n_rows):
    rb = pl.program_id(2)
    lane = lax.broadcasted_iota(jnp.int32, (GRID_W, LANES), 1)
    first = lane < NA_HEAD_DIM

    def window(i):
        r = rb * rows_per_step + i
        r0 = jnp.clip(r - NA_KH // 2, 0, n_rows - NA_KH)
        return r - r0, pl.multiple_of(r0 * GRID_W, GRID_W)

    def scores(i):
        pat, koff = window(i)
        q2 = q_ref[0, i * GRID_W:(i + 1) * GRID_W, :]
        zero = jnp.zeros_like(q2)
        qs = jnp.concatenate([jnp.where(first, q2, zero), jnp.where(first, zero, q2)], axis=0)
        kw = k_ref[0, pl.ds(koff, NA_WIN), :]
        s = lax.dot_general(qs, kw, (((1,), (1,)), ((), ())), preferred_element_type=F32)
        return s + bias_ref[pat, 0]

    def finish(i, s):
        _, koff = window(i)
        vw = v_ref[0, pl.ds(koff, NA_WIN), :]
        m = jnp.max(s, axis=1, keepdims=True)
        p = jnp.exp(s - m)
        l = jnp.sum(p, axis=1, keepdims=True)
        pv = jnp.dot(p.astype(BF16), vw, preferred_element_type=F32) / l
        o = jnp.where(first, pv[:GRID_W], pv[GRID_W:])
        o_ref[0, i * GRID_W:(i + 1) * GRID_W, :] = o.astype(BF16)

    ahead = min(3, rows_per_step)
    pending = [scores(i) for i in range(ahead)]
    for i in range(rows_per_step):
        if i + ahead < rows_per_step:
            pending.append(scores(i + ahead))
        finish(i, pending[i])
        pending[i] = None


def _na_attn(qkv, bias, *, rows_per_step):
    b, s, _ = qkv.shape
    n_rows = s // GRID_W
    tq = rows_per_step * GRID_W
    grid = (NA_PAIRS, b, n_rows // rows_per_step)
    return pl.pallas_call(
        functools.partial(_na_attn_kernel, rows_per_step=rows_per_step, n_rows=n_rows),
        grid=grid,
        in_specs=[
            pl.BlockSpec((1, tq, LANES), lambda p, bi, i: (bi, i, p)),
            pl.BlockSpec((1, s, LANES), lambda p, bi, i: (bi, 0, NA_PAIRS + p)),
            pl.BlockSpec((1, s, LANES), lambda p, bi, i: (bi, 0, 2 * NA_PAIRS + p)),
            pl.BlockSpec((NA_KH, 1, 2 * GRID_W, NA_WIN), lambda p, bi, i: (0, p, 0, 0)),
        ],
        out_specs=pl.BlockSpec((1, tq, LANES), lambda p, bi, i: (bi, i, p)),
        out_shape=jax.ShapeDtypeStruct((b, s, D_MODEL), BF16),
        compiler_params=_params(3),
        name="na_attn",
    )(qkv, qkv, qkv, bias)


def _post_kernel(o_ref, x_ref, wo_ref, g_ref, w1_ref, w2_ref, gf_ref, y_ref, *, ff_chunk, final):
    x = x_ref[0] + jnp.dot(o_ref[0], wo_ref[...], preferred_element_type=F32)
    h = _rms(x, g_ref[...]).astype(BF16)
    y = x
    for c in range(D_FF // ff_chunk):
        a = jnp.dot(h, w1_ref[:, c * ff_chunk:(c + 1) * ff_chunk], preferred_element_type=F32)
        a = jnp.maximum(a, 0.0)
        y = y + jnp.dot((a * a).astype(BF16), w2_ref[c * ff_chunk:(c + 1) * ff_chunk, :],
                        preferred_element_type=F32)
    if final:
        y = _rms(y, gf_ref[...])
    y_ref[0] = y


def _post(o, x, w_o, g_mlp, w1, w2, g_final, *, tm, final):
    b, s, d_o = o.shape
    row = lambda bi, i: (bi, i, 0)
    return pl.pallas_call(
        functools.partial(_post_kernel, ff_chunk=1024, final=final),
        grid=(b, s // tm),
        in_specs=[
            pl.BlockSpec((1, tm, d_o), row),
            pl.BlockSpec((1, tm, D_MODEL), row),
            _const_spec((d_o, D_MODEL)),
            _const_spec((1, D_MODEL)),
            _const_spec((D_MODEL, D_FF)),
            _const_spec((D_FF, D_MODEL)),
            _const_spec((1, D_MODEL)),
        ],
        out_specs=pl.BlockSpec((1, tm, D_MODEL), row),
        out_shape=jax.ShapeDtypeStruct((b, s, D_MODEL), F32),
        compiler_params=_params(2),
        name="post_final" if final else "post",
    )(o, x, w_o.astype(BF16), g_mlp.reshape(1, -1), w1.astype(BF16), w2.astype(BF16),
      g_final.reshape(1, -1))


def _trunk(x, attn_norm, mlp_norm, final_norm, mla_w_dq, mla_q_norm, mla_w_uq, mla_w_dkv,
           mla_kv_norm, mla_w_ukv, mla_w_o, na_w_qkv, na_bias, na_w_o, mlp_w1, mlp_w2):
    s = x.shape[1]
    tm = min(512, s)
    q, kT, v = _mla_proj(x, attn_norm[0], mla_w_dq[0], mla_q_norm[0], mla_w_uq[0], mla_w_dkv[0],
                         mla_kv_norm[0], mla_w_ukv[0], tm=tm)
    o = _mla_attn(q, kT, v, tq=min(1024, s), tk=min(512, s // 2), chunks_per_trip=8)
    x = _post(o, x, mla_w_o[0], mlp_norm[0], mlp_w1[0], mlp_w2[0], final_norm, tm=tm, final=False)
    qkv = _na_proj(x, attn_norm[1], na_w_qkv[0], tm=tm)
    o = _na_attn(qkv, na_bias, rows_per_step=min(16, s // GRID_W))
    return _post(o, x, na_w_o[0], mlp_norm[1], mlp_w1[1], mlp_w2[1], final_norm, tm=tm, final=True)


def kernel(x_prompt, x_sample, attn_norm, mlp_norm, final_norm, mla_w_dq, mla_q_norm, mla_w_uq, mla_w_dkv, mla_kv_norm, mla_w_ukv, mla_w_o, na_w_qkv, na_rpb, na_w_o, mlp_w1, mlp_w2):
    na_bias = _na_bias_table(na_rpb[0])
    args = (attn_norm, mlp_norm, final_norm, mla_w_dq, mla_q_norm, mla_w_uq, mla_w_dkv, mla_kv_norm,
            mla_w_ukv, mla_w_o, na_w_qkv, na_bias, na_w_o, mlp_w1, mlp_w2)
    return (_trunk(x_prompt, *args), _trunk(x_sample, *args))
```

```python
import functools
import math

import numpy as np
import jax
import jax.numpy as jnp
from jax import lax
from jax.experimental import pallas as pl
from jax.experimental.pallas import tpu as pltpu

D_MODEL = 1024
GRID_W = 64
MLA_HEADS = 16
Q_LORA = 384
KV_LORA = 256
QK_NOPE = 128
QK_ROPE = 64
V_HEAD = 128
ROPE_THETA = 10000.0
NA_HEADS = 16
NA_HEAD_DIM = D_MODEL // NA_HEADS
NA_KH = 8
NA_KW = 16
D_FF = 4 * D_MODEL
EPS = 1e-6

LANES = 128
BF16_ROWS = 16
QK_PAD = 2 * LANES
VT_ROWS = V_HEAD + BF16_ROWS
NA_PAIRS = NA_HEADS * NA_HEAD_DIM // LANES
NA_WIN = NA_KH * GRID_W
MASK_VALUE = -1e30
VMEM_LIMIT = 56 * 1024 * 1024

F32 = jnp.float32
BF16 = jnp.bfloat16


def _rms(x, g):
    return x * lax.rsqrt(jnp.mean(x * x, axis=-1, keepdims=True) + EPS) * g


def _const_spec(shape):
    zeros = (0,) * len(shape)
    return pl.BlockSpec(shape, lambda *_: zeros, pipeline_mode=pl.Buffered(1))


def _params(n_axes):
    return pltpu.CompilerParams(
        dimension_semantics=("arbitrary",) * n_axes, vmem_limit_bytes=VMEM_LIMIT)


def _mla_proj_kernel(x_ref, g_ref, wdq_ref, qn_ref, wuqT_ref, wdkvc_ref, wkr_ref, kvn_ref,
                     wuk_ref, wuvT_ref, cosT_ref, sinT_ref, tk0_ref, tk1_ref,
                     qT_ref, k_ref, vT_ref, *, q_scale):
    nt = (((1,), (1,)), ((), ()))
    half = QK_ROPE // 2
    d_qk = QK_NOPE + QK_ROPE
    h = _rms(x_ref[0], g_ref[...]).astype(BF16)
    tm = h.shape[0]
    cq = jnp.dot(h, wdq_ref[...], preferred_element_type=F32)
    cq = _rms(cq, qn_ref[...]).astype(BF16)
    qT = lax.dot_general(wuqT_ref[...], cq, nt, preferred_element_type=F32)
    cos = cosT_ref[...] * q_scale
    sin = sinT_ref[...] * q_scale
    zpad = jnp.zeros((QK_PAD - d_qk, tm), BF16)
    for hd in range(MLA_HEADS):
        src, dst = hd * d_qk, hd * QK_PAD
        x1 = qT[src + QK_NOPE:src + QK_NOPE + half]
        x2 = qT[src + QK_NOPE + half:src + d_qk]
        qT_ref[0, dst:dst + QK_NOPE, :] = (qT[src:src + QK_NOPE] * q_scale).astype(BF16)
        qT_ref[0, dst + QK_NOPE:dst + QK_NOPE + half, :] = (x1 * cos - x2 * sin).astype(BF16)
        qT_ref[0, dst + QK_NOPE + half:dst + d_qk, :] = (x1 * sin + x2 * cos).astype(BF16)
        qT_ref[0, dst + d_qk:dst + QK_PAD, :] = zpad

    ckv = jnp.dot(h, wdkvc_ref[...], preferred_element_type=F32)
    ckv = _rms(ckv, kvn_ref[...]).astype(BF16)
    kn = jnp.dot(ckv, wuk_ref[...], preferred_element_type=F32).astype(BF16)
    kr = jnp.dot(h, wkr_ref[...], preferred_element_type=F32)
    kr = (kr * tk0_ref[...] + pltpu.roll(kr, LANES // 2, axis=1) * tk1_ref[...]).astype(BF16)
    for hd in range(MLA_HEADS):
        k_ref[0, :, hd * QK_PAD:hd * QK_PAD + QK_NOPE] = kn[:, hd * QK_NOPE:(hd + 1) * QK_NOPE]
        k_ref[0, :, hd * QK_PAD + QK_NOPE:(hd + 1) * QK_PAD] = kr
    vT = lax.dot_general(wuvT_ref[...], ckv, nt, preferred_element_type=F32).astype(BF16)
    ones_row = lax.broadcasted_iota(jnp.int32, (VT_ROWS - V_HEAD, tm), 0) == 0
    tail = jnp.where(ones_row, 1.0, 0.0).astype(BF16)
    for hd in range(MLA_HEADS):
        vT_ref[0, hd * VT_ROWS:hd * VT_ROWS + V_HEAD, :] = vT[hd * V_HEAD:(hd + 1) * V_HEAD]
        vT_ref[0, hd * VT_ROWS + V_HEAD:(hd + 1) * VT_ROWS, :] = tail


def _mla_proj(x, g, w_dq, q_norm, w_uq, w_dkv, kv_norm, w_ukv, *, tm):
    b, s, _ = x.shape
    q_scale = (QK_NOPE + QK_ROPE) ** -0.5 * math.log2(math.e)

    half = QK_ROPE // 2
    w_uqT = w_uq.T.astype(BF16)
    wdkv_c = w_dkv[:, :KV_LORA].astype(BF16)
    k1, k2 = w_dkv[:, KV_LORA:KV_LORA + half], w_dkv[:, KV_LORA + half:]
    w_kr = jnp.concatenate([k1, k2, k2, k1], axis=-1).astype(BF16)
    wkv = w_ukv.reshape(KV_LORA, MLA_HEADS, QK_NOPE + V_HEAD)
    w_uk = wkv[..., :QK_NOPE].reshape(KV_LORA, MLA_HEADS * QK_NOPE).astype(BF16)
    w_uvT = wkv[..., QK_NOPE:].reshape(KV_LORA, MLA_HEADS * V_HEAD).T.astype(BF16)

    inv = ROPE_THETA ** (-jnp.arange(0, QK_ROPE, 2, dtype=F32) / QK_ROPE)
    ang = jnp.arange(s, dtype=F32)[:, None] * inv[None, :]
    cos, sin = jnp.cos(ang), jnp.sin(ang)
    zero = jnp.zeros_like(cos)
    t0 = jnp.concatenate([cos, cos, zero, zero], axis=-1)
    t1 = jnp.concatenate([-sin, sin, zero, zero], axis=-1)

    grid = (b, s // tm)
    row = lambda bi, i: (bi, i, 0)
    col = lambda bi, i: (bi, 0, i)
    tab = pl.BlockSpec((tm, LANES), lambda bi, i: (i, 0))
    tabT = pl.BlockSpec((half, tm), lambda bi, i: (0, i))
    return pl.pallas_call(
        functools.partial(_mla_proj_kernel, q_scale=q_scale),
        grid=grid,
        in_specs=[
            pl.BlockSpec((1, tm, D_MODEL), row),
            _const_spec((1, D_MODEL)),
            _const_spec((D_MODEL, Q_LORA)),
            _const_spec((1, Q_LORA)),
            _const_spec((MLA_HEADS * (QK_NOPE + QK_ROPE), Q_LORA)),
            _const_spec((D_MODEL, KV_LORA)),
            _const_spec((D_MODEL, LANES)),
            _const_spec((1, KV_LORA)),
            _const_spec((KV_LORA, MLA_HEADS * QK_NOPE)),
            _const_spec((MLA_HEADS * V_HEAD, KV_LORA)),
            tabT, tabT, tab, tab,
        ],
        out_specs=[
            pl.BlockSpec((1, MLA_HEADS * QK_PAD, tm), col),
            pl.BlockSpec((1, tm, MLA_HEADS * QK_PAD), row),
            pl.BlockSpec((1, MLA_HEADS * VT_ROWS, tm), col),
        ],
        out_shape=[
            jax.ShapeDtypeStruct((b, MLA_HEADS * QK_PAD, s), BF16),
            jax.ShapeDtypeStruct((b, s, MLA_HEADS * QK_PAD), BF16),
            jax.ShapeDtypeStruct((b, MLA_HEADS * VT_ROWS, s), BF16),
        ],
        compiler_params=_params(2),
        name="mla_proj",
    )(x, g.reshape(1, -1), w_dq.astype(BF16), q_norm.reshape(1, -1), w_uqT, wdkv_c, w_kr,
      kv_norm.reshape(1, -1), w_uk, w_uvT, cos.T, sin.T, t0, t1)


def _mla_attn_kernel(qT_ref, k_ref, vT_ref, o_ref, sa_scr, sb_scr, m_scr, acc_scr, *, tk, chunks_per_trip):
    nk = k_ref.shape[1] // tk
    m_scr[...] = jnp.full(m_scr.shape, MASK_VALUE, F32)
    acc_scr[...] = jnp.zeros(acc_scr.shape, F32)
    s_bufs = (sa_scr, sb_scr)

    def scores(j, dst):
        off = pl.multiple_of(j * tk, tk)
        dst[...] = jnp.dot(k_ref[0, pl.ds(off, tk), :], qT_ref[0], preferred_element_type=F32)

    def softmax_pv(src, j):
        off = pl.multiple_of(j * tk, tk)
        m_prev = m_scr[...]
        m_next = jnp.maximum(m_prev, jnp.max(src[...], axis=0, keepdims=True))
        alpha = jnp.exp2(m_prev - m_next)
        p = jnp.exp2(src[...] - m_next).astype(BF16)
        pv = jnp.dot(vT_ref[0, :, pl.ds(off, tk)], p, preferred_element_type=F32)
        acc_scr[...] = alpha * acc_scr[...] + pv
        m_scr[...] = m_next

    def chunk(j, parity, last):
        if not last:
            scores(j + 1, s_bufs[1 - parity])
        softmax_pv(s_bufs[parity], j)

    scores(0, sa_scr)
    n_trips = (nk - 1) // chunks_per_trip

    def body(t, carry):
        for k in range(chunks_per_trip):
            chunk(t * chunks_per_trip + k, k % 2, False)
        return carry

    if n_trips > 0:
        lax.fori_loop(0, n_trips, body, 0)
    for j in range(n_trips * chunks_per_trip, nk):
        chunk(j, j % 2, j == nk - 1)
    oT = acc_scr[:V_HEAD, :] / acc_scr[V_HEAD:V_HEAD + 1, :]
    o_ref[0] = oT.T.astype(BF16)


def _mla_attn(qT, k, vT, *, tq, tk, chunks_per_trip):
    b, s, _ = k.shape
    assert s % tk == 0 and s % tq == 0 and chunks_per_trip % 2 == 0
    grid = (b, MLA_HEADS, s // tq)
    return pl.pallas_call(
        functools.partial(_mla_attn_kernel, tk=tk, chunks_per_trip=chunks_per_trip),
        grid=grid,
        in_specs=[
            pl.BlockSpec((1, QK_PAD, tq), lambda bi, h, i: (bi, h, i)),
            pl.BlockSpec((1, s, QK_PAD), lambda bi, h, i: (bi, 0, h)),
            pl.BlockSpec((1, VT_ROWS, s), lambda bi, h, i: (bi, h, 0)),
        ],
        out_specs=pl.BlockSpec((1, tq, V_HEAD), lambda bi, h, i: (bi, i, h)),
        out_shape=jax.ShapeDtypeStruct((b, s, MLA_HEADS * V_HEAD), BF16),
        scratch_shapes=[pltpu.VMEM((tk, tq), F32), pltpu.VMEM((tk, tq), F32),
                        pltpu.VMEM((1, tq), F32), pltpu.VMEM((VT_ROWS, tq), F32)],
        compiler_params=_params(3),
        name="mla_attn",
    )(qT, k, vT)


def _na_proj_kernel(x_ref, g_ref, w_ref, o_ref, *, q_scale):
    h = _rms(x_ref[0], g_ref[...]).astype(BF16)
    qkv = jnp.dot(h, w_ref[...], preferred_element_type=F32)
    o_ref[0, :, :D_MODEL] = (qkv[:, :D_MODEL] * q_scale).astype(BF16)
    o_ref[0, :, D_MODEL:] = qkv[:, D_MODEL:].astype(BF16)


def _na_proj(x, g, w_qkv, *, tm):
    b, s, _ = x.shape
    row = lambda bi, i: (bi, i, 0)
    return pl.pallas_call(
        functools.partial(_na_proj_kernel, q_scale=NA_HEAD_DIM ** -0.5),
        grid=(b, s // tm),
        in_specs=[pl.BlockSpec((1, tm, D_MODEL), row), _const_spec((1, D_MODEL)),
                  _const_spec((D_MODEL, 3 * D_MODEL))],
        out_specs=pl.BlockSpec((1, tm, 3 * D_MODEL), row),
        out_shape=jax.ShapeDtypeStruct((b, s, 3 * D_MODEL), BF16),
        compiler_params=_params(2),
        name="na_proj",
    )(x, g.reshape(1, -1), w_qkv.astype(BF16))


def _na_bias_table(rpb):
    c = np.arange(GRID_W)[:, None]
    kc = np.arange(GRID_W)[None, :]
    c0 = np.clip(c - NA_KW // 2, 0, GRID_W - NA_KW)
    valid = (kc >= c0) & (kc < c0 + NA_KW)
    rel_col = kc - c + NA_KW - 1
    n_rel = 2 * NA_KW - 1
    onehot = (valid[..., None] & (rel_col[..., None] == np.arange(n_rel))).astype(np.float32)
    band = jnp.einsum("hrm,ckm->hrck", rpb.astype(F32), jnp.asarray(onehot),
                      precision=lax.Precision.HIGHEST)
    band = jnp.where(jnp.asarray(valid), band, MASK_VALUE)
    tbl = jnp.stack([band[:, NA_KH - 1 - pat:2 * NA_KH - 1 - pat] for pat in range(NA_KH)])
    tbl = tbl.transpose(0, 1, 3, 2, 4)
    return tbl.reshape(NA_KH, NA_PAIRS, 2 * GRID_W, NA_WIN)


def _na_attn_kernel(q_ref, k_ref, v_ref, bias_ref, o_ref, *, rows_per_step, n_rows):
    rb = pl.program_id(2)
    lane = lax.broadcasted_iota(jnp.int32, (GRID_W, LANES), 1)
    first = lane < NA_HEAD_DIM

    def window(i):
        r = rb * rows_per_step + i
        r0 = jnp.clip(r - NA_KH // 2, 0, n_rows - NA_KH)
        return r - r0, pl.multiple_of(r0 * GRID_W, GRID_W)

    def scores(i):
        pat, koff = window(i)
        q2 = q_ref[0, i * GRID_W:(i + 1) * GRID_W, :]
        zero = jnp.zeros_like(q2)
        qs = jnp.concatenate([jnp.where(first, q2, zero), jnp.where(first, zero, q2)], axis=0)
        kw = k_ref[0, pl.ds(koff, NA_WIN), :]
        s = lax.dot_general(qs, kw, (((1,), (1,)), ((), ())), preferred_element_type=F32)
        return s + bias_ref[pat, 0]

    def finish(i, s):
        _, koff = window(i)
        vw = v_ref[0, pl.ds(koff, NA_WIN), :]
        m = jnp.max(s, axis=1, keepdims=True)
        p = jnp.exp(s - m)
        l = jnp.sum(p, axis=1, keepdims=True)
        pv = jnp.dot(p.astype(BF16), vw, preferred_element_type=F32) / l
        o = jnp.where(first, pv[:GRID_W], pv[GRID_W:])
        o_ref[0, i * GRID_W:(i + 1) * GRID_W, :] = o.astype(BF16)

    ahead = min(3, rows_per_step)
    pending = [scores(i) for i in range(ahead)]
    for i in range(rows_per_step):
        if i + ahead < rows_per_step:
            pending.append(scores(i + ahead))
        finish(i, pending[i])
        pending[i] = None


def _na_attn(qkv, bias, *, rows_per_step):
    b, s, _ = qkv.shape
    n_rows = s // GRID_W
    tq = rows_per_step * GRID_W
    grid = (NA_PAIRS, b, n_rows // rows_per_step)
    return pl.pallas_call(
        functools.partial(_na_attn_kernel, rows_per_step=rows_per_step, n_rows=n_rows),
        grid=grid,
        in_specs=[
            pl.BlockSpec((1, tq, LANES), lambda p, bi, i: (bi, i, p)),
            pl.BlockSpec((1, s, LANES), lambda p, bi, i: (bi, 0, NA_PAIRS + p)),
            pl.BlockSpec((1, s, LANES), lambda p, bi, i: (bi, 0, 2 * NA_PAIRS + p)),
            pl.BlockSpec((NA_KH, 1, 2 * GRID_W, NA_WIN), lambda p, bi, i: (0, p, 0, 0)),
        ],
        out_specs=pl.BlockSpec((1, tq, LANES), lambda p, bi, i: (bi, i, p)),
        out_shape=jax.ShapeDtypeStruct((b, s, D_MODEL), BF16),
        compiler_params=_params(3),
        name="na_attn",
    )(qkv, qkv, qkv, bias)


def _post_kernel(o_ref, x_ref, wo_ref, g_ref, w1_ref, w2_ref, gf_ref, y_ref, *, ff_chunk, final):
    x = x_ref[0] + jnp.dot(o_ref[0], wo_ref[...], preferred_element_type=F32)
    h = _rms(x, g_ref[...]).astype(BF16)
    y = x
    for c in range(D_FF // ff_chunk):
        a = jnp.dot(h, w1_ref[:, c * ff_chunk:(c + 1) * ff_chunk], preferred_element_type=F32)
        a = jnp.maximum(a, 0.0)
        y = y + jnp.dot((a * a).astype(BF16), w2_ref[c * ff_chunk:(c + 1) * ff_chunk, :],
                        preferred_element_type=F32)
    if final:
        y = _rms(y, gf_ref[...])
    y_ref[0] = y


def _post(o, x, w_o, g_mlp, w1, w2, g_final, *, tm, final):
    b, s, d_o = o.shape
    row = lambda bi, i: (bi, i, 0)
    return pl.pallas_call(
        functools.partial(_post_kernel, ff_chunk=1024, final=final),
        grid=(b, s // tm),
        in_specs=[
            pl.BlockSpec((1, tm, d_o), row),
            pl.BlockSpec((1, tm, D_MODEL), row),
            _const_spec((d_o, D_MODEL)),
            _const_spec((1, D_MODEL)),
            _const_spec((D_MODEL, D_FF)),
            _const_spec((D_FF, D_MODEL)),
            _const_spec((1, D_MODEL)),
        ],
        out_specs=pl.BlockSpec((1, tm, D_MODEL), row),
        out_shape=jax.ShapeDtypeStruct((b, s, D_MODEL), F32),
        compiler_params=_params(2),
        name="post_final" if final else "post",
    )(o, x, w_o.astype(BF16), g_mlp.reshape(1, -1), w1.astype(BF16), w2.astype(BF16),
      g_final.reshape(1, -1))


def _trunk(x, attn_norm, mlp_norm, final_norm, mla_w_dq, mla_q_norm, mla_w_uq, mla_w_dkv,
           mla_kv_norm, mla_w_ukv, mla_w_o, na_w_qkv, na_bias, na_w_o, mlp_w1, mlp_w2):
    s = x.shape[1]
    tm = min(512, s)
    qT, k, vT = _mla_proj(x, attn_norm[0], mla_w_dq[0], mla_q_norm[0], mla_w_uq[0], mla_w_dkv[0],
                          mla_kv_norm[0], mla_w_ukv[0], tm=tm)
    o = _mla_attn(qT, k, vT, tq=min(1024, s), tk=min(512, s // 2), chunks_per_trip=8)
    x = _post(o, x, mla_w_o[0], mlp_norm[0], mlp_w1[0], mlp_w2[0], final_norm, tm=tm, final=False)
    qkv = _na_proj(x, attn_norm[1], na_w_qkv[0], tm=tm)
    o = _na_attn(qkv, na_bias, rows_per_step=min(16, s // GRID_W))
    return _post(o, x, na_w_o[0], mlp_norm[1], mlp_w1[1], mlp_w2[1], final_norm, tm=tm, final=True)


def kernel(x_prompt, x_sample, attn_norm, mlp_norm, final_norm, mla_w_dq, mla_q_norm, mla_w_uq, mla_w_dkv, mla_kv_norm, mla_w_ukv, mla_w_o, na_w_qkv, na_rpb, na_w_o, mlp_w1, mlp_w2):
    na_bias = _na_bias_table(na_rpb[0])
    args = (attn_norm, mlp_norm, final_norm, mla_w_dq, mla_q_norm, mla_w_uq, mla_w_dkv, mla_kv_norm,
            mla_w_ukv, mla_w_o, na_w_qkv, na_bias, na_w_o, mlp_w1, mlp_w2)
    return (_trunk(x_prompt, *args), _trunk(x_sample, *args))
```

```python
import functools
import math

import numpy as np
import jax
import jax.numpy as jnp
from jax import lax
from jax.experimental import pallas as pl
from jax.experimental.pallas import tpu as pltpu

D_MODEL = 1024
GRID_W = 64
MLA_HEADS = 16
Q_LORA = 384
KV_LORA = 256
QK_NOPE = 128
QK_ROPE = 64
V_HEAD = 128
ROPE_THETA = 10000.0
NA_HEADS = 16
NA_HEAD_DIM = D_MODEL // NA_HEADS
NA_KH = 8
NA_KW = 16
D_FF = 4 * D_MODEL
EPS = 1e-6

LANES = 128
MXU_WIDTH = 256
BF16_ROWS = 16
QK_PAD = 2 * LANES
VT_ROWS = V_HEAD + BF16_ROWS
NA_PAIRS = NA_HEADS * NA_HEAD_DIM // LANES
NA_WIN = NA_KH * GRID_W
MASK_VALUE = -1e30
VMEM_LIMIT = 56 * 1024 * 1024

F32 = jnp.float32
BF16 = jnp.bfloat16


def _rms(x, g):
    return x * lax.rsqrt(jnp.mean(x * x, axis=-1, keepdims=True) + EPS) * g


def _const_spec(shape):
    zeros = (0,) * len(shape)
    return pl.BlockSpec(shape, lambda *_: zeros, pipeline_mode=pl.Buffered(1))


def _params(n_axes):
    return pltpu.CompilerParams(
        dimension_semantics=("arbitrary",) * n_axes, vmem_limit_bytes=VMEM_LIMIT)


def _mla_proj_kernel(x_ref, g_ref, wdq_ref, qn_ref, wuqT_ref, wdkvc_ref, wkr_ref, kvn_ref,
                     wuk_ref, wuvT_ref, cosT_ref, sinT_ref, tk0_ref, tk1_ref,
                     qT_ref, k_ref, vT_ref, *, q_scale):
    nt = (((1,), (1,)), ((), ()))
    half = QK_ROPE // 2
    d_qk = QK_NOPE + QK_ROPE
    h = _rms(x_ref[0], g_ref[...]).astype(BF16)
    tm = h.shape[0]
    cq = jnp.dot(h, wdq_ref[...], preferred_element_type=F32)
    cq = _rms(cq, qn_ref[...]).astype(BF16)
    qT = lax.dot_general(wuqT_ref[...], cq, nt, preferred_element_type=F32)
    cos = cosT_ref[...] * q_scale
    sin = sinT_ref[...] * q_scale
    zpad = jnp.zeros((QK_PAD - d_qk, tm), BF16)
    for hd in range(MLA_HEADS):
        src, dst = hd * d_qk, hd * QK_PAD
        x1 = qT[src + QK_NOPE:src + QK_NOPE + half]
        x2 = qT[src + QK_NOPE + half:src + d_qk]
        qT_ref[0, dst:dst + QK_NOPE, :] = (qT[src:src + QK_NOPE] * q_scale).astype(BF16)
        qT_ref[0, dst + QK_NOPE:dst + QK_NOPE + half, :] = (x1 * cos - x2 * sin).astype(BF16)
        qT_ref[0, dst + QK_NOPE + half:dst + d_qk, :] = (x1 * sin + x2 * cos).astype(BF16)
        qT_ref[0, dst + d_qk:dst + QK_PAD, :] = zpad

    ckv = jnp.dot(h, wdkvc_ref[...], preferred_element_type=F32)
    ckv = _rms(ckv, kvn_ref[...]).astype(BF16)
    kn = jnp.dot(ckv, wuk_ref[...], preferred_element_type=F32).astype(BF16)
    kr = jnp.dot(h, wkr_ref[...], preferred_element_type=F32)
    kr = (kr * tk0_ref[...] + pltpu.roll(kr, LANES // 2, axis=1) * tk1_ref[...]).astype(BF16)
    for hd in range(MLA_HEADS):
        k_ref[0, :, hd * QK_PAD:hd * QK_PAD + QK_NOPE] = kn[:, hd * QK_NOPE:(hd + 1) * QK_NOPE]
        k_ref[0, :, hd * QK_PAD + QK_NOPE:(hd + 1) * QK_PAD] = kr
    vT = lax.dot_general(wuvT_ref[...], ckv, nt, preferred_element_type=F32).astype(BF16)
    ones_row = lax.broadcasted_iota(jnp.int32, (VT_ROWS - V_HEAD, tm), 0) == 0
    tail = jnp.where(ones_row, 1.0, 0.0).astype(BF16)
    for hd in range(MLA_HEADS):
        vT_ref[0, hd * VT_ROWS:hd * VT_ROWS + V_HEAD, :] = vT[hd * V_HEAD:(hd + 1) * V_HEAD]
        vT_ref[0, hd * VT_ROWS + V_HEAD:(hd + 1) * VT_ROWS, :] = tail


def _mla_proj(x, g, w_dq, q_norm, w_uq, w_dkv, kv_norm, w_ukv, *, tm):
    b, s, _ = x.shape
    q_scale = (QK_NOPE + QK_ROPE) ** -0.5 * math.log2(math.e)

    half = QK_ROPE // 2
    w_uqT = w_uq.T.astype(BF16)
    wdkv_c = w_dkv[:, :KV_LORA].astype(BF16)
    k1, k2 = w_dkv[:, KV_LORA:KV_LORA + half], w_dkv[:, KV_LORA + half:]
    w_kr = jnp.concatenate([k1, k2, k2, k1], axis=-1).astype(BF16)
    wkv = w_ukv.reshape(KV_LORA, MLA_HEADS, QK_NOPE + V_HEAD)
    w_uk = wkv[..., :QK_NOPE].reshape(KV_LORA, MLA_HEADS * QK_NOPE).astype(BF16)
    w_uvT = wkv[..., QK_NOPE:].reshape(KV_LORA, MLA_HEADS * V_HEAD).T.astype(BF16)

    inv = ROPE_THETA ** (-jnp.arange(0, QK_ROPE, 2, dtype=F32) / QK_ROPE)
    ang = jnp.arange(s, dtype=F32)[:, None] * inv[None, :]
    cos, sin = jnp.cos(ang), jnp.sin(ang)
    zero = jnp.zeros_like(cos)
    t0 = jnp.concatenate([cos, cos, zero, zero], axis=-1)
    t1 = jnp.concatenate([-sin, sin, zero, zero], axis=-1)

    grid = (b, s // tm)
    row = lambda bi, i: (bi, i, 0)
    col = lambda bi, i: (bi, 0, i)
    tab = pl.BlockSpec((tm, LANES), lambda bi, i: (i, 0))
    tabT = pl.BlockSpec((half, tm), lambda bi, i: (0, i))
    return pl.pallas_call(
        functools.partial(_mla_proj_kernel, q_scale=q_scale),
        grid=grid,
        in_specs=[
            pl.BlockSpec((1, tm, D_MODEL), row),
            _const_spec((1, D_MODEL)),
            _const_spec((D_MODEL, Q_LORA)),
            _const_spec((1, Q_LORA)),
            _const_spec((MLA_HEADS * (QK_NOPE + QK_ROPE), Q_LORA)),
            _const_spec((D_MODEL, KV_LORA)),
            _const_spec((D_MODEL, LANES)),
            _const_spec((1, KV_LORA)),
            _const_spec((KV_LORA, MLA_HEADS * QK_NOPE)),
            _const_spec((MLA_HEADS * V_HEAD, KV_LORA)),
            tabT, tabT, tab, tab,
        ],
        out_specs=[
            pl.BlockSpec((1, MLA_HEADS * QK_PAD, tm), col),
            pl.BlockSpec((1, tm, MLA_HEADS * QK_PAD), row),
            pl.BlockSpec((1, MLA_HEADS * VT_ROWS, tm), col),
        ],
        out_shape=[
            jax.ShapeDtypeStruct((b, MLA_HEADS * QK_PAD, s), BF16),
            jax.ShapeDtypeStruct((b, s, MLA_HEADS * QK_PAD), BF16),
            jax.ShapeDtypeStruct((b, MLA_HEADS * VT_ROWS, s), BF16),
        ],
        compiler_params=_params(2),
        name="mla_proj",
    )(x, g.reshape(1, -1), w_dq.astype(BF16), q_norm.reshape(1, -1), w_uqT, wdkv_c, w_kr,
      kv_norm.reshape(1, -1), w_uk, w_uvT, cos.T, sin.T, t0, t1)


def _mla_attn_kernel(qT_ref, k_ref, vT_ref, o_ref, sa_scr, sb_scr, m_scr, acc_scr, *, tk, chunks_per_trip):
    nk = k_ref.shape[1] // tk
    tq = qT_ref.shape[2]
    m_scr[...] = jnp.full(m_scr.shape, MASK_VALUE, F32)
    acc_scr[...] = jnp.zeros(acc_scr.shape, F32)
    s_bufs = (sa_scr, sb_scr)
    groups = [slice(c, c + MXU_WIDTH) for c in range(0, tq, MXU_WIDTH)]

    def scores(j, dst, g):
        off = pl.multiple_of(j * tk, tk)
        dst[:, g] = jnp.dot(k_ref[0, pl.ds(off, tk), :], qT_ref[0, :, g], preferred_element_type=F32)

    def softmax_pv(src, j, g):
        off = pl.multiple_of(j * tk, tk)
        m_prev = m_scr[:, g]
        m_next = jnp.maximum(m_prev, jnp.max(src[:, g], axis=0, keepdims=True))
        alpha = jnp.exp2(m_prev - m_next)
        p = jnp.exp2(src[:, g] - m_next).astype(BF16)
        pv = jnp.dot(vT_ref[0, :, pl.ds(off, tk)], p, preferred_element_type=F32)
        acc_scr[:, g] = alpha * acc_scr[:, g] + pv
        m_scr[:, g] = m_next

    def chunk(j, parity, last):
        for g in groups:
            if not last:
                scores(j + 1, s_bufs[1 - parity], g)
            softmax_pv(s_bufs[parity], j, g)

    for g in groups:
        scores(0, sa_scr, g)
    n_trips = (nk - 1) // chunks_per_trip

    def body(t, carry):
        for k in range(chunks_per_trip):
            chunk(t * chunks_per_trip + k, k % 2, False)
        return carry

    if n_trips > 0:
        lax.fori_loop(0, n_trips, body, 0)
    for j in range(n_trips * chunks_per_trip, nk):
        chunk(j, j % 2, j == nk - 1)
    oT = acc_scr[:V_HEAD, :] / acc_scr[V_HEAD:V_HEAD + 1, :]
    o_ref[0] = oT.T.astype(BF16)


def _mla_attn(qT, k, vT, *, tq, tk, chunks_per_trip):
    b, s, _ = k.shape
    assert s % tk == 0 and s % tq == 0 and chunks_per_trip % 2 == 0
    grid = (b, MLA_HEADS, s // tq)
    return pl.pallas_call(
        functools.partial(_mla_attn_kernel, tk=tk, chunks_per_trip=chunks_per_trip),
        grid=grid,
        in_specs=[
            pl.BlockSpec((1, QK_PAD, tq), lambda bi, h, i: (bi, h, i)),
            pl.BlockSpec((1, s, QK_PAD), lambda bi, h, i: (bi, 0, h)),
            pl.BlockSpec((1, VT_ROWS, s), lambda bi, h, i: (bi, h, 0)),
        ],
        out_specs=pl.BlockSpec((1, tq, V_HEAD), lambda bi, h, i: (bi, i, h)),
        out_shape=jax.ShapeDtypeStruct((b, s, MLA_HEADS * V_HEAD), BF16),
        scratch_shapes=[pltpu.VMEM((tk, tq), F32), pltpu.VMEM((tk, tq), F32),
                        pltpu.VMEM((1, tq), F32), pltpu.VMEM((VT_ROWS, tq), F32)],
        compiler_params=_params(3),
        name="mla_attn",
    )(qT, k, vT)


def _na_proj_kernel(x_ref, g_ref, w_ref, o_ref, *, q_scale):
    h = _rms(x_ref[0], g_ref[...]).astype(BF16)
    qkv = jnp.dot(h, w_ref[...], preferred_element_type=F32)
    o_ref[0, :, :D_MODEL] = (qkv[:, :D_MODEL] * q_scale).astype(BF16)
    o_ref[0, :, D_MODEL:] = qkv[:, D_MODEL:].astype(BF16)


def _na_proj(x, g, w_qkv, *, tm):
    b, s, _ = x.shape
    row = lambda bi, i: (bi, i, 0)
    return pl.pallas_call(
        functools.partial(_na_proj_kernel, q_scale=NA_HEAD_DIM ** -0.5),
        grid=(b, s // tm),
        in_specs=[pl.BlockSpec((1, tm, D_MODEL), row), _const_spec((1, D_MODEL)),
                  _const_spec((D_MODEL, 3 * D_MODEL))],
        out_specs=pl.BlockSpec((1, tm, 3 * D_MODEL), row),
        out_shape=jax.ShapeDtypeStruct((b, s, 3 * D_MODEL), BF16),
        compiler_params=_params(2),
        name="na_proj",
    )(x, g.reshape(1, -1), w_qkv.astype(BF16))


def _na_bias_table(rpb):
    c = np.arange(GRID_W)[:, None]
    kc = np.arange(GRID_W)[None, :]
    c0 = np.clip(c - NA_KW // 2, 0, GRID_W - NA_KW)
    valid = (kc >= c0) & (kc < c0 + NA_KW)
    rel_col = kc - c + NA_KW - 1
    n_rel = 2 * NA_KW - 1
    onehot = (valid[..., None] & (rel_col[..., None] == np.arange(n_rel))).astype(np.float32)
    band = jnp.einsum("hrm,ckm->hrck", rpb.astype(F32), jnp.asarray(onehot),
                      precision=lax.Precision.HIGHEST)
    band = jnp.where(jnp.asarray(valid), band, MASK_VALUE)
    tbl = jnp.stack([band[:, NA_KH - 1 - pat:2 * NA_KH - 1 - pat] for pat in range(NA_KH)])
    tbl = tbl.transpose(0, 1, 3, 2, 4)
    return tbl.reshape(NA_KH, NA_PAIRS, 2 * GRID_W, NA_WIN)


def _na_attn_kernel(q_ref, k_ref, v_ref, bias_ref, o_ref, *, rows_per_step, n_rows):
    rb = pl.program_id(2)
    lane = lax.broadcasted_iota(jnp.int32, (GRID_W, LANES), 1)
    first = lane < NA_HEAD_DIM

    def window(i):
        r = rb * rows_per_step + i
        r0 = jnp.clip(r - NA_KH // 2, 0, n_rows - NA_KH)
        return r - r0, pl.multiple_of(r0 * GRID_W, GRID_W)

    def scores(i):
        pat, koff = window(i)
        q2 = q_ref[0, i * GRID_W:(i + 1) * GRID_W, :]
        zero = jnp.zeros_like(q2)
        qs = jnp.concatenate([jnp.where(first, q2, zero), jnp.where(first, zero, q2)], axis=0)
        kw = k_ref[0, pl.ds(koff, NA_WIN), :]
        s = lax.dot_general(qs, kw, (((1,), (1,)), ((), ())), preferred_element_type=F32)
        return s + bias_ref[pat, 0]

    def finish(i, s):
        _, koff = window(i)
        vw = v_ref[0, pl.ds(koff, NA_WIN), :]
        m = jnp.max(s, axis=1, keepdims=True)
        p = jnp.exp(s - m)
        l = jnp.sum(p, axis=1, keepdims=True)
        pv = jnp.dot(p.astype(BF16), vw, preferred_element_type=F32) / l
        o = jnp.where(first, pv[:GRID_W], pv[GRID_W:])
        o_ref[0, i * GRID_W:(i + 1) * GRID_W, :] = o.astype(BF16)

    ahead = min(3, rows_per_step)
    pending = [scores(i) for i in range(ahead)]
    for i in range(rows_per_step):
        if i + ahead < rows_per_step:
            pending.append(scores(i + ahead))
        finish(i, pending[i])
        pending[i] = None


def _na_attn(qkv, bias, *, rows_per_step):
    b, s, _ = qkv.shape
    n_rows = s // GRID_W
    tq = rows_per_step * GRID_W
    grid = (NA_PAIRS, b, n_rows // rows_per_step)
    return pl.pallas_call(
        functools.partial(_na_attn_kernel, rows_per_step=rows_per_step, n_rows=n_rows),
        grid=grid,
        in_specs=[
            pl.BlockSpec((1, tq, LANES), lambda p, bi, i: (bi, i, p)),
            pl.BlockSpec((1, s, LANES), lambda p, bi, i: (bi, 0, NA_PAIRS + p)),
            pl.BlockSpec((1, s, LANES), lambda p, bi, i: (bi, 0, 2 * NA_PAIRS + p)),
            pl.BlockSpec((NA_KH, 1, 2 * GRID_W, NA_WIN), lambda p, bi, i: (0, p, 0, 0)),
        ],
        out_specs=pl.BlockSpec((1, tq, LANES), lambda p, bi, i: (bi, i, p)),
        out_shape=jax.ShapeDtypeStruct((b, s, D_MODEL), BF16),
        compiler_params=_params(3),
        name="na_attn",
    )(qkv, qkv, qkv, bias)


def _post_kernel(o_ref, x_ref, wo_ref, g_ref, w1_ref, w2_ref, gf_ref, y_ref, *, ff_chunk, final):
    x = x_ref[0] + jnp.dot(o_ref[0], wo_ref[...], preferred_element_type=F32)
    h = _rms(x, g_ref[...]).astype(BF16)
    y = x
    for c in range(D_FF // ff_chunk):
        a = jnp.dot(h, w1_ref[:, c * ff_chunk:(c + 1) * ff_chunk], preferred_element_type=F32)
        a = jnp.maximum(a, 0.0)
        y = y + jnp.dot((a * a).astype(BF16), w2_ref[c * ff_chunk:(c + 1) * ff_chunk, :],
                        preferred_element_type=F32)
    if final:
        y = _rms(y, gf_ref[...])
    y_ref[0] = y


def _post(o, x, w_o, g_mlp, w1, w2, g_final, *, tm, final):
    b, s, d_o = o.shape
    row = lambda bi, i: (bi, i, 0)
    return pl.pallas_call(
        functools.partial(_post_kernel, ff_chunk=1024, final=final),
        grid=(b, s // tm),
        in_specs=[
            pl.BlockSpec((1, tm, d_o), row),
            pl.BlockSpec((1, tm, D_MODEL), row),
            _const_spec((d_o, D_MODEL)),
            _const_spec((1, D_MODEL)),
            _const_spec((D_MODEL, D_FF)),
            _const_spec((D_FF, D_MODEL)),
            _const_spec((1, D_MODEL)),
        ],
        out_specs=pl.BlockSpec((1, tm, D_MODEL), row),
        out_shape=jax.ShapeDtypeStruct((b, s, D_MODEL), F32),
        compiler_params=_params(2),
        name="post_final" if final else "post",
    )(o, x, w_o.astype(BF16), g_mlp.reshape(1, -1), w1.astype(BF16), w2.astype(BF16),
      g_final.reshape(1, -1))


def _trunk(x, attn_norm, mlp_norm, final_norm, mla_w_dq, mla_q_norm, mla_w_uq, mla_w_dkv,
           mla_kv_norm, mla_w_ukv, mla_w_o, na_w_qkv, na_bias, na_w_o, mlp_w1, mlp_w2):
    s = x.shape[1]
    tm = min(512, s)
    qT, k, vT = _mla_proj(x, attn_norm[0], mla_w_dq[0], mla_q_norm[0], mla_w_uq[0], mla_w_dkv[0],
                          mla_kv_norm[0], mla_w_ukv[0], tm=tm)
    o = _mla_attn(qT, k, vT, tq=min(1024, s), tk=min(512, s // 2), chunks_per_trip=8)
    x = _post(o, x, mla_w_o[0], mlp_norm[0], mlp_w1[0], mlp_w2[0], final_norm, tm=tm, final=False)
    qkv = _na_proj(x, attn_norm[1], na_w_qkv[0], tm=tm)
    o = _na_attn(qkv, na_bias, rows_per_step=min(16, s // GRID_W))
    return _post(o, x, na_w_o[0], mlp_norm[1], mlp_w1[1], mlp_w2[1], final_norm, tm=tm, final=True)


def kernel(x_prompt, x_sample, attn_norm, mlp_norm, final_norm, mla_w_dq, mla_q_norm, mla_w_uq, mla_w_dkv, mla_kv_norm, mla_w_ukv, mla_w_o, na_w_qkv, na_rpb, na_w_o, mlp_w1, mlp_w2):
    na_bias = _na_bias_table(na_rpb[0])
    args = (attn_norm, mlp_norm, final_norm, mla_w_dq, mla_q_norm, mla_w_uq, mla_w_dkv, mla_kv_norm,
            mla_w_ukv, mla_w_o, na_w_qkv, na_bias, na_w_o, mlp_w1, mlp_w2)
    return (_trunk(x_prompt, *args), _trunk(x_sample, *args))
```

```python
import functools
import math

import numpy as np
import jax
import jax.numpy as jnp
from jax import lax
from jax.experimental import pallas as pl
from jax.experimental.pallas import tpu as pltpu

D_MODEL = 1024
GRID_W = 64
MLA_HEADS = 16
Q_LORA = 384
KV_LORA = 256
QK_NOPE = 128
QK_ROPE = 64
V_HEAD = 128
ROPE_THETA = 10000.0
NA_HEADS = 16
NA_HEAD_DIM = D_MODEL // NA_HEADS
NA_KH = 8
NA_KW = 16
D_FF = 4 * D_MODEL
EPS = 1e-6

LANES = 128
MXU_WIDTH = 256
BF16_ROWS = 16
QK_PAD = 2 * LANES
VT_ROWS = V_HEAD + BF16_ROWS
NA_PAIRS = NA_HEADS * NA_HEAD_DIM // LANES
NA_WIN = NA_KH * GRID_W
MASK_VALUE = -1e30
VMEM_LIMIT = 56 * 1024 * 1024

F32 = jnp.float32
BF16 = jnp.bfloat16


def _rms(x, g):
    return x * lax.rsqrt(jnp.mean(x * x, axis=-1, keepdims=True) + EPS) * g


def _const_spec(shape):
    zeros = (0,) * len(shape)
    return pl.BlockSpec(shape, lambda *_: zeros, pipeline_mode=pl.Buffered(1))


def _params(n_axes):
    return pltpu.CompilerParams(
        dimension_semantics=("arbitrary",) * n_axes, vmem_limit_bytes=VMEM_LIMIT)


def _mla_proj_kernel(x_ref, g_ref, wdq_ref, qn_ref, wuqT_ref, wdkvc_ref, wkr_ref, kvn_ref,
                     wuk_ref, wuvT_ref, cosT_ref, sinT_ref, tk0_ref, tk1_ref,
                     qT_ref, k_ref, vT_ref, *, q_scale):
    nt = (((1,), (1,)), ((), ()))
    half = QK_ROPE // 2
    d_qk = QK_NOPE + QK_ROPE
    h = _rms(x_ref[0], g_ref[...]).astype(BF16)
    tm = h.shape[0]
    cq = jnp.dot(h, wdq_ref[...], preferred_element_type=F32)
    cq = _rms(cq, qn_ref[...]).astype(BF16)
    qT = lax.dot_general(wuqT_ref[...], cq, nt, preferred_element_type=F32)
    cos = cosT_ref[...] * q_scale
    sin = sinT_ref[...] * q_scale
    zpad = jnp.zeros((QK_PAD - d_qk, tm), BF16)
    for hd in range(MLA_HEADS):
        src, dst = hd * d_qk, hd * QK_PAD
        x1 = qT[src + QK_NOPE:src + QK_NOPE + half]
        x2 = qT[src + QK_NOPE + half:src + d_qk]
        qT_ref[0, dst:dst + QK_NOPE, :] = (qT[src:src + QK_NOPE] * q_scale).astype(BF16)
        qT_ref[0, dst + QK_NOPE:dst + QK_NOPE + half, :] = (x1 * cos - x2 * sin).astype(BF16)
        qT_ref[0, dst + QK_NOPE + half:dst + d_qk, :] = (x1 * sin + x2 * cos).astype(BF16)
        qT_ref[0, dst + d_qk:dst + QK_PAD, :] = zpad

    ckv = jnp.dot(h, wdkvc_ref[...], preferred_element_type=F32)
    ckv = _rms(ckv, kvn_ref[...]).astype(BF16)
    kn = jnp.dot(ckv, wuk_ref[...], preferred_element_type=F32).astype(BF16)
    kr = jnp.dot(h, wkr_ref[...], preferred_element_type=F32)
    kr = (kr * tk0_ref[...] + pltpu.roll(kr, LANES // 2, axis=1) * tk1_ref[...]).astype(BF16)
    for hd in range(MLA_HEADS):
        k_ref[0, :, hd * QK_PAD:hd * QK_PAD + QK_NOPE] = kn[:, hd * QK_NOPE:(hd + 1) * QK_NOPE]
        k_ref[0, :, hd * QK_PAD + QK_NOPE:(hd + 1) * QK_PAD] = kr
    vT = lax.dot_general(wuvT_ref[...], ckv, nt, preferred_element_type=F32).astype(BF16)
    ones_row = lax.broadcasted_iota(jnp.int32, (VT_ROWS - V_HEAD, tm), 0) == 0
    tail = jnp.where(ones_row, 1.0, 0.0).astype(BF16)
    for hd in range(MLA_HEADS):
        vT_ref[0, hd * VT_ROWS:hd * VT_ROWS + V_HEAD, :] = vT[hd * V_HEAD:(hd + 1) * V_HEAD]
        vT_ref[0, hd * VT_ROWS + V_HEAD:(hd + 1) * VT_ROWS, :] = tail


def _mla_proj(x, g, w_dq, q_norm, w_uq, w_dkv, kv_norm, w_ukv, *, tm):
    b, s, _ = x.shape
    q_scale = (QK_NOPE + QK_ROPE) ** -0.5 * math.log2(math.e)

    half = QK_ROPE // 2
    w_uqT = w_uq.T.astype(BF16)
    wdkv_c = w_dkv[:, :KV_LORA].astype(BF16)
    k1, k2 = w_dkv[:, KV_LORA:KV_LORA + half], w_dkv[:, KV_LORA + half:]
    w_kr = jnp.concatenate([k1, k2, k2, k1], axis=-1).astype(BF16)
    wkv = w_ukv.reshape(KV_LORA, MLA_HEADS, QK_NOPE + V_HEAD)
    w_uk = wkv[..., :QK_NOPE].reshape(KV_LORA, MLA_HEADS * QK_NOPE).astype(BF16)
    w_uvT = wkv[..., QK_NOPE:].reshape(KV_LORA, MLA_HEADS * V_HEAD).T.astype(BF16)

    inv = ROPE_THETA ** (-jnp.arange(0, QK_ROPE, 2, dtype=F32) / QK_ROPE)
    ang = jnp.arange(s, dtype=F32)[:, None] * inv[None, :]
    cos, sin = jnp.cos(ang), jnp.sin(ang)
    zero = jnp.zeros_like(cos)
    t0 = jnp.concatenate([cos, cos, zero, zero], axis=-1)
    t1 = jnp.concatenate([-sin, sin, zero, zero], axis=-1)

    grid = (b, s // tm)
    row = lambda bi, i: (bi, i, 0)
    col = lambda bi, i: (bi, 0, i)
    tab = pl.BlockSpec((tm, LANES), lambda bi, i: (i, 0))
    tabT = pl.BlockSpec((half, tm), lambda bi, i: (0, i))
    return pl.pallas_call(
        functools.partial(_mla_proj_kernel, q_scale=q_scale),
        grid=grid,
        in_specs=[
            pl.BlockSpec((1, tm, D_MODEL), row),
            _const_spec((1, D_MODEL)),
            _const_spec((D_MODEL, Q_LORA)),
            _const_spec((1, Q_LORA)),
            _const_spec((MLA_HEADS * (QK_NOPE + QK_ROPE), Q_LORA)),
            _const_spec((D_MODEL, KV_LORA)),
            _const_spec((D_MODEL, LANES)),
            _const_spec((1, KV_LORA)),
            _const_spec((KV_LORA, MLA_HEADS * QK_NOPE)),
            _const_spec((MLA_HEADS * V_HEAD, KV_LORA)),
            tabT, tabT, tab, tab,
        ],
        out_specs=[
            pl.BlockSpec((1, MLA_HEADS * QK_PAD, tm), col),
            pl.BlockSpec((1, tm, MLA_HEADS * QK_PAD), row),
            pl.BlockSpec((1, MLA_HEADS * VT_ROWS, tm), col),
        ],
        out_shape=[
            jax.ShapeDtypeStruct((b, MLA_HEADS * QK_PAD, s), BF16),
            jax.ShapeDtypeStruct((b, s, MLA_HEADS * QK_PAD), BF16),
            jax.ShapeDtypeStruct((b, MLA_HEADS * VT_ROWS, s), BF16),
        ],
        compiler_params=_params(2),
        name="mla_proj",
    )(x, g.reshape(1, -1), w_dq.astype(BF16), q_norm.reshape(1, -1), w_uqT, wdkv_c, w_kr,
      kv_norm.reshape(1, -1), w_uk, w_uvT, cos.T, sin.T, t0, t1)


def _mla_attn_kernel(qT_ref, k_ref, vT_ref, o_ref, sa_scr, sb_scr, m_scr, acc_scr, *, tk, chunks_per_trip):
    nk = k_ref.shape[1] // tk
    tq = qT_ref.shape[2]
    m_scr[...] = jnp.full(m_scr.shape, MASK_VALUE, F32)
    acc_scr[...] = jnp.zeros(acc_scr.shape, F32)
    s_bufs = (sa_scr, sb_scr)
    groups = [slice(c, c + MXU_WIDTH) for c in range(0, tq, MXU_WIDTH)]

    def scores(j, dst, g):
        off = pl.multiple_of(j * tk, tk)
        dst[:, g] = jnp.dot(k_ref[0, pl.ds(off, tk), :], qT_ref[0, :, g], preferred_element_type=F32)

    def softmax_pv(src, j, g):
        off = pl.multiple_of(j * tk, tk)
        m_prev = m_scr[:, g]
        m_next = jnp.maximum(m_prev, jnp.max(src[:, g], axis=0, keepdims=True))
        alpha = jnp.exp2(m_prev - m_next)
        p = jnp.exp2(src[:, g] - m_next).astype(BF16)
        pv = jnp.dot(vT_ref[0, :, pl.ds(off, tk)], p, preferred_element_type=F32)
        acc_scr[:, g] = alpha * acc_scr[:, g] + pv
        m_scr[:, g] = m_next

    def chunk(j, parity, last):
        for g in groups:
            if not last:
                scores(j + 1, s_bufs[1 - parity], g)
            softmax_pv(s_bufs[parity], j, g)

    for g in groups:
        scores(0, sa_scr, g)
    n_trips = (nk - 1) // chunks_per_trip

    def body(t, carry):
        for k in range(chunks_per_trip):
            chunk(t * chunks_per_trip + k, k % 2, False)
        return carry

    if n_trips > 0:
        lax.fori_loop(0, n_trips, body, 0)
    for j in range(n_trips * chunks_per_trip, nk):
        chunk(j, j % 2, j == nk - 1)
    oT = acc_scr[:V_HEAD, :] / acc_scr[V_HEAD:V_HEAD + 1, :]
    o_ref[0] = oT.T.astype(BF16)


def _mla_attn(qT, k, vT, *, tq, tk, chunks_per_trip):
    b, s, _ = k.shape
    assert s % tk == 0 and s % tq == 0 and chunks_per_trip % 2 == 0
    grid = (b, MLA_HEADS, s // tq)
    return pl.pallas_call(
        functools.partial(_mla_attn_kernel, tk=tk, chunks_per_trip=chunks_per_trip),
        grid=grid,
        in_specs=[
            pl.BlockSpec((1, QK_PAD, tq), lambda bi, h, i: (bi, h, i)),
            pl.BlockSpec((1, s, QK_PAD), lambda bi, h, i: (bi, 0, h)),
            pl.BlockSpec((1, VT_ROWS, s), lambda bi, h, i: (bi, h, 0)),
        ],
        out_specs=pl.BlockSpec((1, tq, V_HEAD), lambda bi, h, i: (bi, i, h)),
        out_shape=jax.ShapeDtypeStruct((b, s, MLA_HEADS * V_HEAD), BF16),
        scratch_shapes=[pltpu.VMEM((tk, tq), F32), pltpu.VMEM((tk, tq), F32),
                        pltpu.VMEM((1, tq), F32), pltpu.VMEM((VT_ROWS, tq), F32)],
        compiler_params=_params(3),
        name="mla_attn",
    )(qT, k, vT)


def _na_proj_kernel(x_ref, g_ref, w_ref, o_ref, *, q_scale):
    h = _rms(x_ref[0], g_ref[...]).astype(BF16)
    qkv = jnp.dot(h, w_ref[...], preferred_element_type=F32)
    o_ref[0, :, :D_MODEL] = (qkv[:, :D_MODEL] * q_scale).astype(BF16)
    o_ref[0, :, D_MODEL:] = qkv[:, D_MODEL:].astype(BF16)


def _na_proj(x, g, w_qkv, *, tm):
    b, s, _ = x.shape
    row = lambda bi, i: (bi, i, 0)
    return pl.pallas_call(
        functools.partial(_na_proj_kernel, q_scale=NA_HEAD_DIM ** -0.5),
        grid=(b, s // tm),
        in_specs=[pl.BlockSpec((1, tm, D_MODEL), row), _const_spec((1, D_MODEL)),
                  _const_spec((D_MODEL, 3 * D_MODEL))],
        out_specs=pl.BlockSpec((1, tm, 3 * D_MODEL), row),
        out_shape=jax.ShapeDtypeStruct((b, s, 3 * D_MODEL), BF16),
        compiler_params=_params(2),
        name="na_proj",
    )(x, g.reshape(1, -1), w_qkv.astype(BF16))


def _na_bias_table(rpb):
    c = np.arange(GRID_W)[:, None]
    kc = np.arange(GRID_W)[None, :]
    c0 = np.clip(c - NA_KW // 2, 0, GRID_W - NA_KW)
    valid = (kc >= c0) & (kc < c0 + NA_KW)
    rel_col = kc - c + NA_KW - 1
    n_rel = 2 * NA_KW - 1
    onehot = (valid[..., None] & (rel_col[..., None] == np.arange(n_rel))).astype(np.float32)
    band = jnp.einsum("hrm,ckm->hrck", rpb.astype(F32), jnp.asarray(onehot),
                      precision=lax.Precision.HIGHEST)
    band = jnp.where(jnp.asarray(valid), band, MASK_VALUE)
    tbl = jnp.stack([band[:, NA_KH - 1 - pat:2 * NA_KH - 1 - pat] for pat in range(NA_KH)])
    tbl = tbl.transpose(0, 1, 3, 2, 4)
    return tbl.reshape(NA_KH, NA_PAIRS, 2 * GRID_W, NA_WIN)


def _na_attn_kernel(q_ref, k_ref, v_ref, bias_ref, o_ref, *, rows_per_step, n_rows):
    rb = pl.program_id(2)
    lane = lax.broadcasted_iota(jnp.int32, (GRID_W, LANES), 1)
    first = lane < NA_HEAD_DIM

    def window(i):
        r = rb * rows_per_step + i
        r0 = jnp.clip(r - NA_KH // 2, 0, n_rows - NA_KH)
        return r - r0, pl.multiple_of(r0 * GRID_W, GRID_W)

    def scores(i):
        pat, koff = window(i)
        q2 = q_ref[0, i * GRID_W:(i + 1) * GRID_W, :]
        zero = jnp.zeros_like(q2)
        qs = jnp.concatenate([jnp.where(first, q2, zero), jnp.where(first, zero, q2)], axis=0)
        kw = k_ref[0, pl.ds(koff, NA_WIN), :]
        s = lax.dot_general(qs, kw, (((1,), (1,)), ((), ())), preferred_element_type=F32)
        return s + bias_ref[pat, 0]

    def finish(i, s):
        _, koff = window(i)
        vw = v_ref[0, pl.ds(koff, NA_WIN), :]
        m = jnp.max(s, axis=1, keepdims=True)
        p = jnp.exp(s - m)
        l = jnp.sum(p, axis=1, keepdims=True)
        pv = jnp.dot(p.astype(BF16), vw, preferred_element_type=F32) / l
        o = jnp.where(first, pv[:GRID_W], pv[GRID_W:])
        o_ref[0, i * GRID_W:(i + 1) * GRID_W, :] = o.astype(BF16)

    ahead = min(3, rows_per_step)
    pending = [scores(i) for i in range(ahead)]
    for i in range(rows_per_step):
        if i + ahead < rows_per_step:
            pending.append(scores(i + ahead))
        finish(i, pending[i])
        pending[i] = None


def _na_attn(qkv, bias, *, rows_per_step):
    b, s, _ = qkv.shape
    n_rows = s // GRID_W
    tq = rows_per_step * GRID_W
    grid = (NA_PAIRS, b, n_rows // rows_per_step)
    return pl.pallas_call(
        functools.partial(_na_attn_kernel, rows_per_step=rows_per_step, n_rows=n_rows),
        grid=grid,
        in_specs=[
            pl.BlockSpec((1, tq, LANES), lambda p, bi, i: (bi, i, p)),
            pl.BlockSpec((1, s, LANES), lambda p, bi, i: (bi, 0, NA_PAIRS + p)),
            pl.BlockSpec((1, s, LANES), lambda p, bi, i: (bi, 0, 2 * NA_PAIRS + p)),
            pl.BlockSpec((NA_KH, 1, 2 * GRID_W, NA_WIN), lambda p, bi, i: (0, p, 0, 0)),
        ],
        out_specs=pl.BlockSpec((1, tq, LANES), lambda p, bi, i: (bi, i, p)),
        out_shape=jax.ShapeDtypeStruct((b, s, D_MODEL), BF16),
        compiler_params=_params(3),
        name="na_attn",
    )(qkv, qkv, qkv, bias)


def _post_kernel(o_ref, x_ref, wo_ref, g_ref, w1_ref, w2_ref, gf_ref, y_ref, *, ff_chunk, final):
    x = x_ref[0] + jnp.dot(o_ref[0], wo_ref[...], preferred_element_type=F32)
    h = _rms(x, g_ref[...]).astype(BF16)
    y = x
    for c in range(D_FF // ff_chunk):
        a = jnp.dot(h, w1_ref[:, c * ff_chunk:(c + 1) * ff_chunk], preferred_element_type=F32)
        a = jnp.maximum(a, 0.0)
        y = y + jnp.dot((a * a).astype(BF16), w2_ref[c * ff_chunk:(c + 1) * ff_chunk, :],
                        preferred_element_type=F32)
    if final:
        y = _rms(y, gf_ref[...])
    y_ref[0] = y


def _post(o, x, w_o, g_mlp, w1, w2, g_final, *, tm, final):
    b, s, d_o = o.shape
    row = lambda bi, i: (bi, i, 0)
    return pl.pallas_call(
        functools.partial(_post_kernel, ff_chunk=1024, final=final),
        grid=(b, s // tm),
        in_specs=[
            pl.BlockSpec((1, tm, d_o), row),
            pl.BlockSpec((1, tm, D_MODEL), row),
            _const_spec((d_o, D_MODEL)),
            _const_spec((1, D_MODEL)),
            _const_spec((D_MODEL, D_FF)),
            _const_spec((D_FF, D_MODEL)),
            _const_spec((1, D_MODEL)),
        ],
        out_specs=pl.BlockSpec((1, tm, D_MODEL), row),
        out_shape=jax.ShapeDtypeStruct((b, s, D_MODEL), F32),
        compiler_params=_params(2),
        name="post_final" if final else "post",
    )(o, x, w_o.astype(BF16), g_mlp.reshape(1, -1), w1.astype(BF16), w2.astype(BF16),
      g_final.reshape(1, -1))


def _trunk(x, attn_norm, mlp_norm, final_norm, mla_w_dq, mla_q_norm, mla_w_uq, mla_w_dkv,
           mla_kv_norm, mla_w_ukv, mla_w_o, na_w_qkv, na_bias, na_w_o, mlp_w1, mlp_w2):
    s = x.shape[1]
    tm = min(512, s)
    qT, k, vT = _mla_proj(x, attn_norm[0], mla_w_dq[0], mla_q_norm[0], mla_w_uq[0], mla_w_dkv[0],
                          mla_kv_norm[0], mla_w_ukv[0], tm=tm)
    tq, tk = (1024, 1024) if s >= 8192 else (min(2048, s), min(512, s // 2))
    o = _mla_attn(qT, k, vT, tq=tq, tk=tk, chunks_per_trip=4096 // tk)
    x = _post(o, x, mla_w_o[0], mlp_norm[0], mlp_w1[0], mlp_w2[0], final_norm, tm=tm, final=False)
    qkv = _na_proj(x, attn_norm[1], na_w_qkv[0], tm=tm)
    o = _na_attn(qkv, na_bias, rows_per_step=min(16, s // GRID_W))
    return _post(o, x, na_w_o[0], mlp_norm[1], mlp_w1[1], mlp_w2[1], final_norm, tm=tm, final=True)


def kernel(x_prompt, x_sample, attn_norm, mlp_norm, final_norm, mla_w_dq, mla_q_norm, mla_w_uq, mla_w_dkv, mla_kv_norm, mla_w_ukv, mla_w_o, na_w_qkv, na_rpb, na_w_o, mlp_w1, mlp_w2):
    na_bias = _na_bias_table(na_rpb[0])
    args = (attn_norm, mlp_norm, final_norm, mla_w_dq, mla_q_norm, mla_w_uq, mla_w_dkv, mla_kv_norm,
            mla_w_ukv, mla_w_o, na_w_qkv, na_bias, na_w_o, mlp_w1, mlp_w2)
    return (_trunk(x_prompt, *args), _trunk(x_sample, *args))
```

```python
import functools
import math

import numpy as np
import jax
import jax.numpy as jnp
from jax import lax
from jax.experimental import pallas as pl
from jax.experimental.pallas import tpu as pltpu

D_MODEL = 1024
GRID_W = 64
MLA_HEADS = 16
Q_LORA = 384
KV_LORA = 256
QK_NOPE = 128
QK_ROPE = 64
V_HEAD = 128
ROPE_THETA = 10000.0
NA_HEADS = 16
NA_HEAD_DIM = D_MODEL // NA_HEADS
NA_KH = 8
NA_KW = 16
D_FF = 4 * D_MODEL
EPS = 1e-6

LANES = 128
MXU_WIDTH = 256
BF16_ROWS = 16
QK_PAD = 2 * LANES
VT_ROWS = V_HEAD + BF16_ROWS
NA_PAIRS = NA_HEADS * NA_HEAD_DIM // LANES
NA_WIN_ROWS = NA_KH + 2
NA_WIN = NA_WIN_ROWS * GRID_W
NA_PATTERNS = 5
MASK_VALUE = -1e30
VMEM_LIMIT = 56 * 1024 * 1024

F32 = jnp.float32
BF16 = jnp.bfloat16


def _rms(x, g):
    return x * lax.rsqrt(jnp.mean(x * x, axis=-1, keepdims=True) + EPS) * g


def _const_spec(shape):
    zeros = (0,) * len(shape)
    return pl.BlockSpec(shape, lambda *_: zeros, pipeline_mode=pl.Buffered(1))


def _params(n_axes):
    return pltpu.CompilerParams(
        dimension_semantics=("arbitrary",) * n_axes, vmem_limit_bytes=VMEM_LIMIT)


def _mla_proj_kernel(x_ref, g_ref, wdq_ref, qn_ref, wuqT_ref, wdkvc_ref, wkr_ref, kvn_ref,
                     wuk_ref, wuvT_ref, cosT_ref, sinT_ref, tk0_ref, tk1_ref,
                     qT_ref, k_ref, vT_ref, *, q_scale):
    nt = (((1,), (1,)), ((), ()))
    half = QK_ROPE // 2
    d_qk = QK_NOPE + QK_ROPE
    h = _rms(x_ref[0], g_ref[...]).astype(BF16)
    tm = h.shape[0]
    cq = jnp.dot(h, wdq_ref[...], preferred_element_type=F32)
    cq = _rms(cq, qn_ref[...]).astype(BF16)
    qT = lax.dot_general(wuqT_ref[...], cq, nt, preferred_element_type=F32)
    cos = cosT_ref[...] * q_scale
    sin = sinT_ref[...] * q_scale
    zpad = jnp.zeros((QK_PAD - d_qk, tm), BF16)
    for hd in range(MLA_HEADS):
        src, dst = hd * d_qk, hd * QK_PAD
        x1 = qT[src + QK_NOPE:src + QK_NOPE + half]
        x2 = qT[src + QK_NOPE + half:src + d_qk]
        qT_ref[0, dst:dst + QK_NOPE, :] = (qT[src:src + QK_NOPE] * q_scale).astype(BF16)
        qT_ref[0, dst + QK_NOPE:dst + QK_NOPE + half, :] = (x1 * cos - x2 * sin).astype(BF16)
        qT_ref[0, dst + QK_NOPE + half:dst + d_qk, :] = (x1 * sin + x2 * cos).astype(BF16)
        qT_ref[0, dst + d_qk:dst + QK_PAD, :] = zpad

    ckv = jnp.dot(h, wdkvc_ref[...], preferred_element_type=F32)
    ckv = _rms(ckv, kvn_ref[...]).astype(BF16)
    kn = jnp.dot(ckv, wuk_ref[...], preferred_element_type=F32).astype(BF16)
    kr = jnp.dot(h, wkr_ref[...], preferred_element_type=F32)
    kr = (kr * tk0_ref[...] + pltpu.roll(kr, LANES // 2, axis=1) * tk1_ref[...]).astype(BF16)
    for hd in range(MLA_HEADS):
        k_ref[0, :, hd * QK_PAD:hd * QK_PAD + QK_NOPE] = kn[:, hd * QK_NOPE:(hd + 1) * QK_NOPE]
        k_ref[0, :, hd * QK_PAD + QK_NOPE:(hd + 1) * QK_PAD] = kr
    vT = lax.dot_general(wuvT_ref[...], ckv, nt, preferred_element_type=F32).astype(BF16)
    ones_row = lax.broadcasted_iota(jnp.int32, (VT_ROWS - V_HEAD, tm), 0) == 0
    tail = jnp.where(ones_row, 1.0, 0.0).astype(BF16)
    for hd in range(MLA_HEADS):
        vT_ref[0, hd * VT_ROWS:hd * VT_ROWS + V_HEAD, :] = vT[hd * V_HEAD:(hd + 1) * V_HEAD]
        vT_ref[0, hd * VT_ROWS + V_HEAD:(hd + 1) * VT_ROWS, :] = tail


def _mla_proj(x, g, w_dq, q_norm, w_uq, w_dkv, kv_norm, w_ukv, *, tm):
    b, s, _ = x.shape
    q_scale = (QK_NOPE + QK_ROPE) ** -0.5 * math.log2(math.e)

    half = QK_ROPE // 2
    w_uqT = w_uq.T.astype(BF16)
    wdkv_c = w_dkv[:, :KV_LORA].astype(BF16)
    k1, k2 = w_dkv[:, KV_LORA:KV_LORA + half], w_dkv[:, KV_LORA + half:]
    w_kr = jnp.concatenate([k1, k2, k2, k1], axis=-1).astype(BF16)
    wkv = w_ukv.reshape(KV_LORA, MLA_HEADS, QK_NOPE + V_HEAD)
    w_uk = wkv[..., :QK_NOPE].reshape(KV_LORA, MLA_HEADS * QK_NOPE).astype(BF16)
    w_uvT = wkv[..., QK_NOPE:].reshape(KV_LORA, MLA_HEADS * V_HEAD).T.astype(BF16)

    inv = ROPE_THETA ** (-jnp.arange(0, QK_ROPE, 2, dtype=F32) / QK_ROPE)
    ang = jnp.arange(s, dtype=F32)[:, None] * inv[None, :]
    cos, sin = jnp.cos(ang), jnp.sin(ang)
    zero = jnp.zeros_like(cos)
    t0 = jnp.concatenate([cos, cos, zero, zero], axis=-1)
    t1 = jnp.concatenate([-sin, sin, zero, zero], axis=-1)

    grid = (b, s // tm)
    row = lambda bi, i: (bi, i, 0)
    col = lambda bi, i: (bi, 0, i)
    tab = pl.BlockSpec((tm, LANES), lambda bi, i: (i, 0))
    tabT = pl.BlockSpec((half, tm), lambda bi, i: (0, i))
    return pl.pallas_call(
        functools.partial(_mla_proj_kernel, q_scale=q_scale),
        grid=grid,
        in_specs=[
            pl.BlockSpec((1, tm, D_MODEL), row),
            _const_spec((1, D_MODEL)),
            _const_spec((D_MODEL, Q_LORA)),
            _const_spec((1, Q_LORA)),
            _const_spec((MLA_HEADS * (QK_NOPE + QK_ROPE), Q_LORA)),
            _const_spec((D_MODEL, KV_LORA)),
            _const_spec((D_MODEL, LANES)),
            _const_spec((1, KV_LORA)),
            _const_spec((KV_LORA, MLA_HEADS * QK_NOPE)),
            _const_spec((MLA_HEADS * V_HEAD, KV_LORA)),
            tabT, tabT, tab, tab,
        ],
        out_specs=[
            pl.BlockSpec((1, MLA_HEADS * QK_PAD, tm), col),
            pl.BlockSpec((1, tm, MLA_HEADS * QK_PAD), row),
            pl.BlockSpec((1, MLA_HEADS * VT_ROWS, tm), col),
        ],
        out_shape=[
            jax.ShapeDtypeStruct((b, MLA_HEADS * QK_PAD, s), BF16),
            jax.ShapeDtypeStruct((b, s, MLA_HEADS * QK_PAD), BF16),
            jax.ShapeDtypeStruct((b, MLA_HEADS * VT_ROWS, s), BF16),
        ],
        compiler_params=_params(2),
        name="mla_proj",
    )(x, g.reshape(1, -1), w_dq.astype(BF16), q_norm.reshape(1, -1), w_uqT, wdkv_c, w_kr,
      kv_norm.reshape(1, -1), w_uk, w_uvT, cos.T, sin.T, t0, t1)


def _mla_attn_kernel(qT_ref, k_ref, vT_ref, o_ref, sa_scr, sb_scr, m_scr, acc_scr, *, tk, chunks_per_trip):
    nk = k_ref.shape[1] // tk
    tq = qT_ref.shape[2]
    m_scr[...] = jnp.full(m_scr.shape, MASK_VALUE, F32)
    acc_scr[...] = jnp.zeros(acc_scr.shape, F32)
    s_bufs = (sa_scr, sb_scr)
    groups = [slice(c, c + MXU_WIDTH) for c in range(0, tq, MXU_WIDTH)]

    def scores(j, dst, g):
        off = pl.multiple_of(j * tk, tk)
        dst[:, g] = jnp.dot(k_ref[0, pl.ds(off, tk), :], qT_ref[0, :, g], preferred_element_type=F32)

    def softmax_pv(src, j, g):
        off = pl.multiple_of(j * tk, tk)
        m_prev = m_scr[:, g]
        m_next = jnp.maximum(m_prev, jnp.max(src[:, g], axis=0, keepdims=True))
        alpha = jnp.exp2(m_prev - m_next)
        p = jnp.exp2(src[:, g] - m_next).astype(BF16)
        pv = jnp.dot(vT_ref[0, :, pl.ds(off, tk)], p, preferred_element_type=F32)
        acc_scr[:, g] = alpha * acc_scr[:, g] + pv
        m_scr[:, g] = m_next

    def chunk(j, parity, last):
        for g in groups:
            if not last:
                scores(j + 1, s_bufs[1 - parity], g)
            softmax_pv(s_bufs[parity], j, g)

    for g in groups:
        scores(0, sa_scr, g)
    n_trips = (nk - 1) // chunks_per_trip

    def body(t, carry):
        for k in range(chunks_per_trip):
            chunk(t * chunks_per_trip + k, k % 2, False)
        return carry

    if n_trips > 0:
        lax.fori_loop(0, n_trips, body, 0)
    for j in range(n_trips * chunks_per_trip, nk):
        chunk(j, j % 2, j == nk - 1)
    oT = acc_scr[:V_HEAD, :] / acc_scr[V_HEAD:V_HEAD + 1, :]
    o_ref[0] = oT.T.astype(BF16)


def _mla_attn(qT, k, vT, *, tq, tk, chunks_per_trip):
    b, s, _ = k.shape
    assert s % tk == 0 and s % tq == 0 and chunks_per_trip % 2 == 0
    grid = (b, MLA_HEADS, s // tq)
    return pl.pallas_call(
        functools.partial(_mla_attn_kernel, tk=tk, chunks_per_trip=chunks_per_trip),
        grid=grid,
        in_specs=[
            pl.BlockSpec((1, QK_PAD, tq), lambda bi, h, i: (bi, h, i)),
            pl.BlockSpec((1, s, QK_PAD), lambda bi, h, i: (bi, 0, h)),
            pl.BlockSpec((1, VT_ROWS, s), lambda bi, h, i: (bi, h, 0)),
        ],
        out_specs=pl.BlockSpec((1, tq, V_HEAD), lambda bi, h, i: (bi, i, h)),
        out_shape=jax.ShapeDtypeStruct((b, s, MLA_HEADS * V_HEAD), BF16),
        scratch_shapes=[pltpu.VMEM((tk, tq), F32), pltpu.VMEM((tk, tq), F32),
                        pltpu.VMEM((1, tq), F32), pltpu.VMEM((VT_ROWS, tq), F32)],
        compiler_params=_params(3),
        name="mla_attn",
    )(qT, k, vT)


def _na_proj_kernel(x_ref, g_ref, wqT_ref, wk_ref, wvT_ref, qT_ref, k_ref, vT_ref, *, q_scale):
    nt = (((1,), (1,)), ((), ()))
    h = _rms(x_ref[0], g_ref[...]).astype(BF16)
    tm = h.shape[0]
    qT = lax.dot_general(wqT_ref[...], h, nt, preferred_element_type=F32)
    qT_ref[0] = (qT * q_scale).astype(BF16)
    k_ref[0] = jnp.dot(h, wk_ref[...], preferred_element_type=F32).astype(BF16)
    vT = lax.dot_general(wvT_ref[...], h, nt, preferred_element_type=F32).astype(BF16)
    ones_row = lax.broadcasted_iota(jnp.int32, (VT_ROWS - LANES, tm), 0) == 0
    tail = jnp.where(ones_row, 1.0, 0.0).astype(BF16)
    for p in range(NA_PAIRS):
        vT_ref[0, p * VT_ROWS:p * VT_ROWS + LANES, :] = vT[p * LANES:(p + 1) * LANES]
        vT_ref[0, p * VT_ROWS + LANES:(p + 1) * VT_ROWS, :] = tail


def _na_proj(x, g, w_qkv, *, tm):
    b, s, _ = x.shape
    row = lambda bi, i: (bi, i, 0)
    col = lambda bi, i: (bi, 0, i)
    w_qT = w_qkv[:, :D_MODEL].T.astype(BF16)
    w_k = w_qkv[:, D_MODEL:2 * D_MODEL].astype(BF16)
    w_vT = w_qkv[:, 2 * D_MODEL:].T.astype(BF16)
    return pl.pallas_call(
        functools.partial(_na_proj_kernel, q_scale=NA_HEAD_DIM ** -0.5 * math.log2(math.e)),
        grid=(b, s // tm),
        in_specs=[pl.BlockSpec((1, tm, D_MODEL), row), _const_spec((1, D_MODEL)),
                  _const_spec((D_MODEL, D_MODEL)), _const_spec((D_MODEL, D_MODEL)),
                  _const_spec((D_MODEL, D_MODEL))],
        out_specs=[pl.BlockSpec((1, D_MODEL, tm), col), pl.BlockSpec((1, tm, D_MODEL), row),
                   pl.BlockSpec((1, NA_PAIRS * VT_ROWS, tm), col)],
        out_shape=[jax.ShapeDtypeStruct((b, D_MODEL, s), BF16),
                   jax.ShapeDtypeStruct((b, s, D_MODEL), BF16),
                   jax.ShapeDtypeStruct((b, NA_PAIRS * VT_ROWS, s), BF16)],
        compiler_params=_params(2),
        name="na_proj",
    )(x, g.reshape(1, -1), w_qT, w_k, w_vT)


def _na_bias_table(rpb):
    c = np.arange(GRID_W)[:, None]
    kc = np.arange(GRID_W)[None, :]
    c0 = np.clip(c - NA_KW // 2, 0, GRID_W - NA_KW)
    valid = (kc >= c0) & (kc < c0 + NA_KW)
    rel_col = kc - c + NA_KW - 1
    n_rel = 2 * NA_KW - 1
    onehot = (valid[..., None] & (rel_col[..., None] == np.arange(n_rel))).astype(np.float32)
    band = jnp.einsum("hrm,ckm->hrck", rpb.astype(F32), jnp.asarray(onehot),
                      precision=lax.Precision.HIGHEST)
    band = jnp.where(jnp.asarray(valid), band * math.log2(math.e), MASK_VALUE)
    bandT = band.transpose(0, 1, 3, 2)
    pad = NA_KH
    bandT = jnp.pad(bandT, ((0, 0), (pad, pad), (0, 0), (0, 0)), constant_values=MASK_VALUE)
    j = np.arange(NA_WIN_ROWS)
    tables = []
    for pat in range(0, 2 * NA_PATTERNS, 2):
        per_row = []
        for rq in range(2):
            start = NA_KH - 1 - pat - rq + pad
            sl = bandT[:, start:start + NA_WIN_ROWS]
            j0 = {0: 0, 2: 0, 4: rq, 6: 2, 8: 2}[pat]
            in_rows = (j >= j0) & (j < j0 + NA_KH)
            per_row.append(jnp.where(jnp.asarray(in_rows)[None, :, None, None], sl, MASK_VALUE))
        t = jnp.stack(per_row, axis=3)
        t = t.reshape(NA_PAIRS, 2, NA_WIN, 2 * GRID_W).transpose(0, 2, 1, 3)
        tables.append(t.reshape(NA_PAIRS, NA_WIN, 4 * GRID_W))
    return jnp.stack(tables)


def _na_attn_kernel(qT_ref, k_ref, vT_ref, bias_ref, o_ref, *, pairs_per_step, n_rows):
    blk = pl.program_id(2)
    feature = lax.broadcasted_iota(jnp.int32, (LANES, LANES), 0)
    first = feature < NA_HEAD_DIM

    def window(a):
        r = (blk * pairs_per_step + a) * 2
        ws = jnp.clip(r - NA_KH // 2, 0, n_rows - NA_WIN_ROWS)
        return lax.shift_right_logical(r - ws, 1), pl.multiple_of(ws * GRID_W, LANES)

    def scores(a):
        pat, koff = window(a)
        q2 = qT_ref[0, :, a * LANES:(a + 1) * LANES]
        zero = jnp.zeros_like(q2)
        rhs = jnp.concatenate([jnp.where(first, q2, zero), jnp.where(first, zero, q2)], axis=1)
        kw = k_ref[0, pl.ds(koff, NA_WIN), :]
        s = jnp.dot(kw, rhs, preferred_element_type=F32)
        return s + bias_ref[pat, 0]

    def finish(a, s):
        _, koff = window(a)
        m = jnp.max(s, axis=0, keepdims=True)
        p = jnp.exp2(s - m).astype(BF16)
        pv = jnp.dot(vT_ref[0, :, pl.ds(koff, NA_WIN)], p, preferred_element_type=F32)
        o2 = pv[:LANES] / pv[LANES:LANES + 1]
        oT = jnp.where(first, o2[:, :LANES], o2[:, LANES:])
        o_ref[0, a * LANES:(a + 1) * LANES, :] = oT.T.astype(BF16)

    ahead = min(3, pairs_per_step)
    pending = [scores(a) for a in range(ahead)]
    for a in range(pairs_per_step):
        if a + ahead < pairs_per_step:
            pending.append(scores(a + ahead))
        finish(a, pending[a])
        pending[a] = None


def _na_attn(qT, k, vT, bias, *, pairs_per_step):
    b, s, _ = k.shape
    n_rows = s // GRID_W
    assert n_rows % (2 * pairs_per_step) == 0 and n_rows >= NA_WIN_ROWS
    tq = 2 * pairs_per_step * GRID_W
    grid = (NA_PAIRS, b, s // tq)
    return pl.pallas_call(
        functools.partial(_na_attn_kernel, pairs_per_step=pairs_per_step, n_rows=n_rows),
        grid=grid,
        in_specs=[
            pl.BlockSpec((1, LANES, tq), lambda p, bi, i: (bi, p, i)),
            pl.BlockSpec((1, s, LANES), lambda p, bi, i: (bi, 0, p)),
            pl.BlockSpec((1, VT_ROWS, s), lambda p, bi, i: (bi, p, 0)),
            pl.BlockSpec((NA_PATTERNS, 1, NA_WIN, 4 * GRID_W), lambda p, bi, i: (0, p, 0, 0)),
        ],
        out_specs=pl.BlockSpec((1, tq, LANES), lambda p, bi, i: (bi, i, p)),
        out_shape=jax.ShapeDtypeStruct((b, s, D_MODEL), BF16),
        compiler_params=_params(3),
        name="na_attn",
    )(qT, k, vT, bias)


def _post_kernel(o_ref, x_ref, wo_ref, g_ref, w1_ref, w2_ref, gf_ref, y_ref, *, ff_chunk, final):
    x = x_ref[0] + jnp.dot(o_ref[0], wo_ref[...], preferred_element_type=F32)
    h = _rms(x, g_ref[...]).astype(BF16)
    y = x
    for c in range(D_FF // ff_chunk):
        a = jnp.dot(h, w1_ref[:, c * ff_chunk:(c + 1) * ff_chunk], preferred_element_type=F32)
        a = jnp.maximum(a, 0.0)
        y = y + jnp.dot((a * a).astype(BF16), w2_ref[c * ff_chunk:(c + 1) * ff_chunk, :],
                        preferred_element_type=F32)
    if final:
        y = _rms(y, gf_ref[...])
    y_ref[0] = y


def _post(o, x, w_o, g_mlp, w1, w2, g_final, *, tm, final):
    b, s, d_o = o.shape
    row = lambda bi, i: (bi, i, 0)
    return pl.pallas_call(
        functools.partial(_post_kernel, ff_chunk=1024, final=final),
        grid=(b, s // tm),
        in_specs=[
            pl.BlockSpec((1, tm, d_o), row),
            pl.BlockSpec((1, tm, D_MODEL), row),
            _const_spec((d_o, D_MODEL)),
            _const_spec((1, D_MODEL)),
            _const_spec((D_MODEL, D_FF)),
            _const_spec((D_FF, D_MODEL)),
            _const_spec((1, D_MODEL)),
        ],
        out_specs=pl.BlockSpec((1, tm, D_MODEL), row),
        out_shape=jax.ShapeDtypeStruct((b, s, D_MODEL), F32),
        compiler_params=_params(2),
        name="post_final" if final else "post",
    )(o, x, w_o.astype(BF16), g_mlp.reshape(1, -1), w1.astype(BF16), w2.astype(BF16),
      g_final.reshape(1, -1))


def _trunk(x, attn_norm, mlp_norm, final_norm, mla_w_dq, mla_q_norm, mla_w_uq, mla_w_dkv,
           mla_kv_norm, mla_w_ukv, mla_w_o, na_w_qkv, na_bias, na_w_o, mlp_w1, mlp_w2):
    s = x.shape[1]
    tm = min(512, s)
    qT, k, vT = _mla_proj(x, attn_norm[0], mla_w_dq[0], mla_q_norm[0], mla_w_uq[0], mla_w_dkv[0],
                          mla_kv_norm[0], mla_w_ukv[0], tm=tm)
    tq, tk = (1024, 1024) if s >= 8192 else (min(2048, s), min(512, s // 2))
    o = _mla_attn(qT, k, vT, tq=tq, tk=tk, chunks_per_trip=4096 // tk)
    x = _post(o, x, mla_w_o[0], mlp_norm[0], mlp_w1[0], mlp_w2[0], final_norm, tm=tm, final=False)
    qT, k, vT = _na_proj(x, attn_norm[1], na_w_qkv[0], tm=tm)
    o = _na_attn(qT, k, vT, na_bias, pairs_per_step=min(16, s // (2 * GRID_W)))
    return _post(o, x, na_w_o[0], mlp_norm[1], mlp_w1[1], mlp_w2[1], final_norm, tm=tm, final=True)


def kernel(x_prompt, x_sample, attn_norm, mlp_norm, final_norm, mla_w_dq, mla_q_norm, mla_w_uq, mla_w_dkv, mla_kv_norm, mla_w_ukv, mla_w_o, na_w_qkv, na_rpb, na_w_o, mlp_w1, mlp_w2):
    na_bias = _na_bias_table(na_rpb[0])
    args = (attn_norm, mlp_norm, final_norm, mla_w_dq, mla_q_norm, mla_w_uq, mla_w_dkv, mla_kv_norm,
            mla_w_ukv, mla_w_o, na_w_qkv, na_bias, na_w_o, mlp_w1, mlp_w2)
    return (_trunk(x_prompt, *args), _trunk(x_sample, *args))
```

```python
import functools
import math

import numpy as np
import jax
import jax.numpy as jnp
from jax import lax
from jax.experimental import pallas as pl
from jax.experimental.pallas import tpu as pltpu

D_MODEL = 1024
GRID_W = 64
MLA_HEADS = 16
Q_LORA = 384
KV_LORA = 256
QK_NOPE = 128
QK_ROPE = 64
V_HEAD = 128
ROPE_THETA = 10000.0
NA_HEADS = 16
NA_HEAD_DIM = D_MODEL // NA_HEADS
NA_KH = 8
NA_KW = 16
D_FF = 4 * D_MODEL
EPS = 1e-6

LANES = 128
MXU_WIDTH = 256
BF16_ROWS = 16
QK_PAD = 2 * LANES
VT_ROWS = V_HEAD + BF16_ROWS
NA_PAIRS = NA_HEADS * NA_HEAD_DIM // LANES
NA_WIN = NA_KH * GRID_W
MASK_VALUE = -1e30
VMEM_LIMIT = 56 * 1024 * 1024

F32 = jnp.float32
BF16 = jnp.bfloat16


def _rms(x, g):
    return x * lax.rsqrt(jnp.mean(x * x, axis=-1, keepdims=True) + EPS) * g


def _const_spec(shape):
    zeros = (0,) * len(shape)
    return pl.BlockSpec(shape, lambda *_: zeros, pipeline_mode=pl.Buffered(1))


def _params(n_axes):
    return pltpu.CompilerParams(
        dimension_semantics=("arbitrary",) * n_axes, vmem_limit_bytes=VMEM_LIMIT)


def _mla_proj_kernel(x_ref, g_ref, wdq_ref, qn_ref, wuqT_ref, wdkvc_ref, wkr_ref, kvn_ref,
                     wuk_ref, wuvT_ref, cosT_ref, sinT_ref, tk0_ref, tk1_ref,
                     qT_ref, k_ref, vT_ref, *, q_scale):
    nt = (((1,), (1,)), ((), ()))
    half = QK_ROPE // 2
    d_qk = QK_NOPE + QK_ROPE
    h = _rms(x_ref[0], g_ref[...]).astype(BF16)
    tm = h.shape[0]
    cq = jnp.dot(h, wdq_ref[...], preferred_element_type=F32)
    cq = _rms(cq, qn_ref[...]).astype(BF16)
    qT = lax.dot_general(wuqT_ref[...], cq, nt, preferred_element_type=F32)
    cos = cosT_ref[...] * q_scale
    sin = sinT_ref[...] * q_scale
    zpad = jnp.zeros((QK_PAD - d_qk, tm), BF16)
    for hd in range(MLA_HEADS):
        src, dst = hd * d_qk, hd * QK_PAD
        x1 = qT[src + QK_NOPE:src + QK_NOPE + half]
        x2 = qT[src + QK_NOPE + half:src + d_qk]
        qT_ref[0, dst:dst + QK_NOPE, :] = (qT[src:src + QK_NOPE] * q_scale).astype(BF16)
        qT_ref[0, dst + QK_NOPE:dst + QK_NOPE + half, :] = (x1 * cos - x2 * sin).astype(BF16)
        qT_ref[0, dst + QK_NOPE + half:dst + d_qk, :] = (x1 * sin + x2 * cos).astype(BF16)
        qT_ref[0, dst + d_qk:dst + QK_PAD, :] = zpad

    ckv = jnp.dot(h, wdkvc_ref[...], preferred_element_type=F32)
    ckv = _rms(ckv, kvn_ref[...]).astype(BF16)
    kn = jnp.dot(ckv, wuk_ref[...], preferred_element_type=F32).astype(BF16)
    kr = jnp.dot(h, wkr_ref[...], preferred_element_type=F32)
    kr = (kr * tk0_ref[...] + pltpu.roll(kr, LANES // 2, axis=1) * tk1_ref[...]).astype(BF16)
    for hd in range(MLA_HEADS):
        k_ref[0, :, hd * QK_PAD:hd * QK_PAD + QK_NOPE] = kn[:, hd * QK_NOPE:(hd + 1) * QK_NOPE]
        k_ref[0, :, hd * QK_PAD + QK_NOPE:(hd + 1) * QK_PAD] = kr
    vT = lax.dot_general(wuvT_ref[...], ckv, nt, preferred_element_type=F32).astype(BF16)
    ones_row = lax.broadcasted_iota(jnp.int32, (VT_ROWS - V_HEAD, tm), 0) == 0
    tail = jnp.where(ones_row, 1.0, 0.0).astype(BF16)
    for hd in range(MLA_HEADS):
        vT_ref[0, hd * VT_ROWS:hd * VT_ROWS + V_HEAD, :] = vT[hd * V_HEAD:(hd + 1) * V_HEAD]
        vT_ref[0, hd * VT_ROWS + V_HEAD:(hd + 1) * VT_ROWS, :] = tail


def _mla_proj(x, g, w_dq, q_norm, w_uq, w_dkv, kv_norm, w_ukv, *, tm):
    b, s, _ = x.shape
    q_scale = (QK_NOPE + QK_ROPE) ** -0.5 * math.log2(math.e)

    half = QK_ROPE // 2
    w_uqT = w_uq.T.astype(BF16)
    wdkv_c = w_dkv[:, :KV_LORA].astype(BF16)
    k1, k2 = w_dkv[:, KV_LORA:KV_LORA + half], w_dkv[:, KV_LORA + half:]
    w_kr = jnp.concatenate([k1, k2, k2, k1], axis=-1).astype(BF16)
    wkv = w_ukv.reshape(KV_LORA, MLA_HEADS, QK_NOPE + V_HEAD)
    w_uk = wkv[..., :QK_NOPE].reshape(KV_LORA, MLA_HEADS * QK_NOPE).astype(BF16)
    w_uvT = wkv[..., QK_NOPE:].reshape(KV_LORA, MLA_HEADS * V_HEAD).T.astype(BF16)

    inv = ROPE_THETA ** (-jnp.arange(0, QK_ROPE, 2, dtype=F32) / QK_ROPE)
    ang = jnp.arange(s, dtype=F32)[:, None] * inv[None, :]
    cos, sin = jnp.cos(ang), jnp.sin(ang)
    zero = jnp.zeros_like(cos)
    t0 = jnp.concatenate([cos, cos, zero, zero], axis=-1)
    t1 = jnp.concatenate([-sin, sin, zero, zero], axis=-1)

    grid = (b, s // tm)
    row = lambda bi, i: (bi, i, 0)
    col = lambda bi, i: (bi, 0, i)
    tab = pl.BlockSpec((tm, LANES), lambda bi, i: (i, 0))
    tabT = pl.BlockSpec((half, tm), lambda bi, i: (0, i))
    return pl.pallas_call(
        functools.partial(_mla_proj_kernel, q_scale=q_scale),
        grid=grid,
        in_specs=[
            pl.BlockSpec((1, tm, D_MODEL), row),
            _const_spec((1, D_MODEL)),
            _const_spec((D_MODEL, Q_LORA)),
            _const_spec((1, Q_LORA)),
            _const_spec((MLA_HEADS * (QK_NOPE + QK_ROPE), Q_LORA)),
            _const_spec((D_MODEL, KV_LORA)),
            _const_spec((D_MODEL, LANES)),
            _const_spec((1, KV_LORA)),
            _const_spec((KV_LORA, MLA_HEADS * QK_NOPE)),
            _const_spec((MLA_HEADS * V_HEAD, KV_LORA)),
            tabT, tabT, tab, tab,
        ],
        out_specs=[
            pl.BlockSpec((1, MLA_HEADS * QK_PAD, tm), col),
            pl.BlockSpec((1, tm, MLA_HEADS * QK_PAD), row),
            pl.BlockSpec((1, MLA_HEADS * VT_ROWS, tm), col),
        ],
        out_shape=[
            jax.ShapeDtypeStruct((b, MLA_HEADS * QK_PAD, s), BF16),
            jax.ShapeDtypeStruct((b, s, MLA_HEADS * QK_PAD), BF16),
            jax.ShapeDtypeStruct((b, MLA_HEADS * VT_ROWS, s), BF16),
        ],
        compiler_params=_params(2),
        name="mla_proj",
    )(x, g.reshape(1, -1), w_dq.astype(BF16), q_norm.reshape(1, -1), w_uqT, wdkv_c, w_kr,
      kv_norm.reshape(1, -1), w_uk, w_uvT, cos.T, sin.T, t0, t1)


def _mla_attn_kernel(qT_ref, k_ref, vT_ref, o_ref, sa_scr, sb_scr, m_scr, acc_scr, *, tk, chunks_per_trip):
    nk = k_ref.shape[1] // tk
    tq = qT_ref.shape[2]
    m_scr[...] = jnp.full(m_scr.shape, MASK_VALUE, F32)
    acc_scr[...] = jnp.zeros(acc_scr.shape, F32)
    s_bufs = (sa_scr, sb_scr)
    groups = [slice(c, c + MXU_WIDTH) for c in range(0, tq, MXU_WIDTH)]

    def scores(j, dst, g):
        off = pl.multiple_of(j * tk, tk)
        dst[:, g] = jnp.dot(k_ref[0, pl.ds(off, tk), :], qT_ref[0, :, g], preferred_element_type=F32)

    def softmax_pv(src, j, g):
        off = pl.multiple_of(j * tk, tk)
        m_prev = m_scr[:, g]
        m_next = jnp.maximum(m_prev, jnp.max(src[:, g], axis=0, keepdims=True))
        alpha = jnp.exp2(m_prev - m_next)
        p = jnp.exp2(src[:, g] - m_next).astype(BF16)
        pv = jnp.dot(vT_ref[0, :, pl.ds(off, tk)], p, preferred_element_type=F32)
        acc_scr[:, g] = alpha * acc_scr[:, g] + pv
        m_scr[:, g] = m_next

    def chunk(j, parity, last):
        for g in groups:
            if not last:
                scores(j + 1, s_bufs[1 - parity], g)
            softmax_pv(s_bufs[parity], j, g)

    for g in groups:
        scores(0, sa_scr, g)
    n_trips = (nk - 1) // chunks_per_trip

    def body(t, carry):
        for k in range(chunks_per_trip):
            chunk(t * chunks_per_trip + k, k % 2, False)
        return carry

    if n_trips > 0:
        lax.fori_loop(0, n_trips, body, 0)
    for j in range(n_trips * chunks_per_trip, nk):
        chunk(j, j % 2, j == nk - 1)
    oT = acc_scr[:V_HEAD, :] / acc_scr[V_HEAD:V_HEAD + 1, :]
    o_ref[0] = oT.T.astype(BF16)


def _mla_attn(qT, k, vT, *, tq, tk, chunks_per_trip):
    b, s, _ = k.shape
    assert s % tk == 0 and s % tq == 0 and chunks_per_trip % 2 == 0
    grid = (b, MLA_HEADS, s // tq)
    return pl.pallas_call(
        functools.partial(_mla_attn_kernel, tk=tk, chunks_per_trip=chunks_per_trip),
        grid=grid,
        in_specs=[
            pl.BlockSpec((1, QK_PAD, tq), lambda bi, h, i: (bi, h, i)),
            pl.BlockSpec((1, s, QK_PAD), lambda bi, h, i: (bi, 0, h)),
            pl.BlockSpec((1, VT_ROWS, s), lambda bi, h, i: (bi, h, 0)),
        ],
        out_specs=pl.BlockSpec((1, tq, V_HEAD), lambda bi, h, i: (bi, i, h)),
        out_shape=jax.ShapeDtypeStruct((b, s, MLA_HEADS * V_HEAD), BF16),
        scratch_shapes=[pltpu.VMEM((tk, tq), F32), pltpu.VMEM((tk, tq), F32),
                        pltpu.VMEM((1, tq), F32), pltpu.VMEM((VT_ROWS, tq), F32)],
        compiler_params=_params(3),
        name="mla_attn",
    )(qT, k, vT)


def _na_proj_kernel(x_ref, g_ref, w_ref, o_ref, *, q_scale):
    h = _rms(x_ref[0], g_ref[...]).astype(BF16)
    qkv = jnp.dot(h, w_ref[...], preferred_element_type=F32)
    o_ref[0, :, :D_MODEL] = (qkv[:, :D_MODEL] * q_scale).astype(BF16)
    o_ref[0, :, D_MODEL:] = qkv[:, D_MODEL:].astype(BF16)


def _na_proj(x, g, w_qkv, *, tm):
    b, s, _ = x.shape
    row = lambda bi, i: (bi, i, 0)
    return pl.pallas_call(
        functools.partial(_na_proj_kernel, q_scale=NA_HEAD_DIM ** -0.5),
        grid=(b, s // tm),
        in_specs=[pl.BlockSpec((1, tm, D_MODEL), row), _const_spec((1, D_MODEL)),
                  _const_spec((D_MODEL, 3 * D_MODEL))],
        out_specs=pl.BlockSpec((1, tm, 3 * D_MODEL), row),
        out_shape=jax.ShapeDtypeStruct((b, s, 3 * D_MODEL), BF16),
        compiler_params=_params(2),
        name="na_proj",
    )(x, g.reshape(1, -1), w_qkv.astype(BF16))


def _na_bias_table(rpb):
    c = np.arange(GRID_W)[:, None]
    kc = np.arange(GRID_W)[None, :]
    c0 = np.clip(c - NA_KW // 2, 0, GRID_W - NA_KW)
    valid = (kc >= c0) & (kc < c0 + NA_KW)
    rel_col = kc - c + NA_KW - 1
    n_rel = 2 * NA_KW - 1
    onehot = (valid[..., None] & (rel_col[..., None] == np.arange(n_rel))).astype(np.float32)
    band = jnp.einsum("hrm,ckm->hrck", rpb.astype(F32), jnp.asarray(onehot),
                      precision=lax.Precision.HIGHEST)
    band = jnp.where(jnp.asarray(valid), band, MASK_VALUE)
    tbl = jnp.stack([band[:, NA_KH - 1 - pat:2 * NA_KH - 1 - pat] for pat in range(NA_KH)])
    tbl = tbl.transpose(0, 1, 3, 2, 4)
    return tbl.reshape(NA_KH, NA_PAIRS, 2 * GRID_W, NA_WIN)


def _na_attn_kernel(q_ref, k_ref, v_ref, bias_ref, o_ref, *, rows_per_step, n_rows):
    rb = pl.program_id(2)
    lane = lax.broadcasted_iota(jnp.int32, (GRID_W, LANES), 1)
    first = lane < NA_HEAD_DIM

    def window(i):
        r = rb * rows_per_step + i
        r0 = jnp.clip(r - NA_KH // 2, 0, n_rows - NA_KH)
        return r - r0, pl.multiple_of(r0 * GRID_W, GRID_W)

    def scores(i):
        pat, koff = window(i)
        q2 = q_ref[0, i * GRID_W:(i + 1) * GRID_W, :]
        zero = jnp.zeros_like(q2)
        qs = jnp.concatenate([jnp.where(first, q2, zero), jnp.where(first, zero, q2)], axis=0)
        kw = k_ref[0, pl.ds(koff, NA_WIN), :]
        s = lax.dot_general(qs, kw, (((1,), (1,)), ((), ())), preferred_element_type=F32)
        return s + bias_ref[pat, 0]

    def finish(i, s):
        _, koff = window(i)
        vw = v_ref[0, pl.ds(koff, NA_WIN), :]
        m = jnp.max(s, axis=1, keepdims=True)
        p = jnp.exp(s - m)
        l = jnp.sum(p, axis=1, keepdims=True)
        pv = jnp.dot(p.astype(BF16), vw, preferred_element_type=F32) / l
        o = jnp.where(first, pv[:GRID_W], pv[GRID_W:])
        o_ref[0, i * GRID_W:(i + 1) * GRID_W, :] = o.astype(BF16)

    ahead = min(3, rows_per_step)
    pending = [scores(i) for i in range(ahead)]
    for i in range(rows_per_step):
        if i + ahead < rows_per_step:
            pending.append(scores(i + ahead))
        finish(i, pending[i])
        pending[i] = None


def _na_attn(qkv, bias, *, rows_per_step):
    b, s, _ = qkv.shape
    n_rows = s // GRID_W
    tq = rows_per_step * GRID_W
    grid = (NA_PAIRS, b, n_rows // rows_per_step)
    return pl.pallas_call(
        functools.partial(_na_attn_kernel, rows_per_step=rows_per_step, n_rows=n_rows),
        grid=grid,
        in_specs=[
            pl.BlockSpec((1, tq, LANES), lambda p, bi, i: (bi, i, p)),
            pl.BlockSpec((1, s, LANES), lambda p, bi, i: (bi, 0, NA_PAIRS + p)),
            pl.BlockSpec((1, s, LANES), lambda p, bi, i: (bi, 0, 2 * NA_PAIRS + p)),
            pl.BlockSpec((NA_KH, 1, 2 * GRID_W, NA_WIN), lambda p, bi, i: (0, p, 0, 0)),
        ],
        out_specs=pl.BlockSpec((1, tq, LANES), lambda p, bi, i: (bi, i, p)),
        out_shape=jax.ShapeDtypeStruct((b, s, D_MODEL), BF16),
        compiler_params=_params(3),
        name="na_attn",
    )(qkv, qkv, qkv, bias)


def _post_kernel(o_ref, x_ref, wo_ref, g_ref, w1_ref, w2_ref, gf_ref, y_ref, *, ff_chunk, final):
    x = x_ref[0] + jnp.dot(o_ref[0], wo_ref[...], preferred_element_type=F32)
    h = _rms(x, g_ref[...]).astype(BF16)
    y = x
    for c in range(D_FF // ff_chunk):
        a = jnp.dot(h, w1_ref[:, c * ff_chunk:(c + 1) * ff_chunk], preferred_element_type=F32)
        a = jnp.maximum(a, 0.0)
        y = y + jnp.dot((a * a).astype(BF16), w2_ref[c * ff_chunk:(c + 1) * ff_chunk, :],
                        preferred_element_type=F32)
    if final:
        y = _rms(y, gf_ref[...])
    y_ref[0] = y


def _post(o, x, w_o, g_mlp, w1, w2, g_final, *, tm, final):
    b, s, d_o = o.shape
    row = lambda bi, i: (bi, i, 0)
    return pl.pallas_call(
        functools.partial(_post_kernel, ff_chunk=1024, final=final),
        grid=(b, s // tm),
        in_specs=[
            pl.BlockSpec((1, tm, d_o), row),
            pl.BlockSpec((1, tm, D_MODEL), row),
            _const_spec((d_o, D_MODEL)),
            _const_spec((1, D_MODEL)),
            _const_spec((D_MODEL, D_FF)),
            _const_spec((D_FF, D_MODEL)),
            _const_spec((1, D_MODEL)),
        ],
        out_specs=pl.BlockSpec((1, tm, D_MODEL), row),
        out_shape=jax.ShapeDtypeStruct((b, s, D_MODEL), F32),
        compiler_params=_params(2),
        name="post_final" if final else "post",
    )(o, x, w_o.astype(BF16), g_mlp.reshape(1, -1), w1.astype(BF16), w2.astype(BF16),
      g_final.reshape(1, -1))


def _trunk(x, attn_norm, mlp_norm, final_norm, mla_w_dq, mla_q_norm, mla_w_uq, mla_w_dkv,
           mla_kv_norm, mla_w_ukv, mla_w_o, na_w_qkv, na_bias, na_w_o, mlp_w1, mlp_w2):
    s = x.shape[1]
    tm = min(512, s)
    qT, k, vT = _mla_proj(x, attn_norm[0], mla_w_dq[0], mla_q_norm[0], mla_w_uq[0], mla_w_dkv[0],
                          mla_kv_norm[0], mla_w_ukv[0], tm=tm)
    tq, tk = (1024, 1024) if s >= 8192 else (min(2048, s), min(512, s // 2))
    o = _mla_attn(qT, k, vT, tq=tq, tk=tk, chunks_per_trip=4096 // tk)
    x = _post(o, x, mla_w_o[0], mlp_norm[0], mlp_w1[0], mlp_w2[0], final_norm, tm=tm, final=False)
    qkv = _na_proj(x, attn_norm[1], na_w_qkv[0], tm=tm)
    o = _na_attn(qkv, na_bias, rows_per_step=min(32, s // GRID_W))
    return _post(o, x, na_w_o[0], mlp_norm[1], mlp_w1[1], mlp_w2[1], final_norm, tm=tm, final=True)


def kernel(x_prompt, x_sample, attn_norm, mlp_norm, final_norm, mla_w_dq, mla_q_norm, mla_w_uq, mla_w_dkv, mla_kv_norm, mla_w_ukv, mla_w_o, na_w_qkv, na_rpb, na_w_o, mlp_w1, mlp_w2):
    na_bias = _na_bias_table(na_rpb[0])
    args = (attn_norm, mlp_norm, final_norm, mla_w_dq, mla_q_norm, mla_w_uq, mla_w_dkv, mla_kv_norm,
            mla_w_ukv, mla_w_o, na_w_qkv, na_bias, na_w_o, mlp_w1, mlp_w2)
    return (_trunk(x_prompt, *args), _trunk(x_sample, *args))
```

```python
import functools
import math

import numpy as np
import jax
import jax.numpy as jnp
from jax import lax
from jax.experimental import pallas as pl
from jax.experimental.pallas import tpu as pltpu

D_MODEL = 1024
GRID_W = 64
MLA_HEADS = 16
Q_LORA = 384
KV_LORA = 256
QK_NOPE = 128
QK_ROPE = 64
V_HEAD = 128
ROPE_THETA = 10000.0
NA_HEADS = 16
NA_HEAD_DIM = D_MODEL // NA_HEADS
NA_KH = 8
NA_KW = 16
D_FF = 4 * D_MODEL
EPS = 1e-6

LANES = 128
MXU_WIDTH = 256
BF16_ROWS = 16
QK_PAD = 2 * LANES
VT_ROWS = V_HEAD + BF16_ROWS
NA_PAIRS = NA_HEADS * NA_HEAD_DIM // LANES
NA_WIN = NA_KH * GRID_W
MASK_VALUE = -1e30
VMEM_LIMIT = 56 * 1024 * 1024

F32 = jnp.float32
BF16 = jnp.bfloat16


def _rms(x, g):
    return x * lax.rsqrt(jnp.mean(x * x, axis=-1, keepdims=True) + EPS) * g


def _const_spec(shape):
    zeros = (0,) * len(shape)
    return pl.BlockSpec(shape, lambda *_: zeros, pipeline_mode=pl.Buffered(1))


def _params(n_axes):
    return pltpu.CompilerParams(
        dimension_semantics=("arbitrary",) * n_axes, vmem_limit_bytes=VMEM_LIMIT)


def _mla_proj_kernel(x_ref, g_ref, wdq_ref, qn_ref, wuqT_ref, wdkvc_ref, wkr_ref, kvn_ref,
                     wuk_ref, wuvT_ref, cosT_ref, sinT_ref, tk0_ref, tk1_ref,
                     qT_ref, k_ref, vT_ref, *, q_scale):
    nt = (((1,), (1,)), ((), ()))
    half = QK_ROPE // 2
    d_qk = QK_NOPE + QK_ROPE
    h = _rms(x_ref[0], g_ref[...]).astype(BF16)
    tm = h.shape[0]
    cq = jnp.dot(h, wdq_ref[...], preferred_element_type=F32)
    cq = _rms(cq, qn_ref[...]).astype(BF16)
    qT = lax.dot_general(wuqT_ref[...], cq, nt, preferred_element_type=F32)
    cos = cosT_ref[...] * q_scale
    sin = sinT_ref[...] * q_scale
    zpad = jnp.zeros((QK_PAD - d_qk, tm), BF16)
    for hd in range(MLA_HEADS):
        src, dst = hd * d_qk, hd * QK_PAD
        x1 = qT[src + QK_NOPE:src + QK_NOPE + half]
        x2 = qT[src + QK_NOPE + half:src + d_qk]
        qT_ref[0, dst:dst + QK_NOPE, :] = (qT[src:src + QK_NOPE] * q_scale).astype(BF16)
        qT_ref[0, dst + QK_NOPE:dst + QK_NOPE + half, :] = (x1 * cos - x2 * sin).astype(BF16)
        qT_ref[0, dst + QK_NOPE + half:dst + d_qk, :] = (x1 * sin + x2 * cos).astype(BF16)
        qT_ref[0, dst + d_qk:dst + QK_PAD, :] = zpad

    ckv = jnp.dot(h, wdkvc_ref[...], preferred_element_type=F32)
    ckv = _rms(ckv, kvn_ref[...]).astype(BF16)
    kn = jnp.dot(ckv, wuk_ref[...], preferred_element_type=F32).astype(BF16)
    kr = jnp.dot(h, wkr_ref[...], preferred_element_type=F32)
    kr = (kr * tk0_ref[...] + pltpu.roll(kr, LANES // 2, axis=1) * tk1_ref[...]).astype(BF16)
    for hd in range(MLA_HEADS):
        k_ref[0, :, hd * QK_PAD:hd * QK_PAD + QK_NOPE] = kn[:, hd * QK_NOPE:(hd + 1) * QK_NOPE]
        k_ref[0, :, hd * QK_PAD + QK_NOPE:(hd + 1) * QK_PAD] = kr
    vT = lax.dot_general(wuvT_ref[...], ckv, nt, preferred_element_type=F32).astype(BF16)
    ones_row = lax.broadcasted_iota(jnp.int32, (VT_ROWS - V_HEAD, tm), 0) == 0
    tail = jnp.where(ones_row, 1.0, 0.0).astype(BF16)
    for hd in range(MLA_HEADS):
        vT_ref[0, hd * VT_ROWS:hd * VT_ROWS + V_HEAD, :] = vT[hd * V_HEAD:(hd + 1) * V_HEAD]
        vT_ref[0, hd * VT_ROWS + V_HEAD:(hd + 1) * VT_ROWS, :] = tail


def _mla_proj(x, g, w_dq, q_norm, w_uq, w_dkv, kv_norm, w_ukv, *, tm):
    b, s, _ = x.shape
    q_scale = (QK_NOPE + QK_ROPE) ** -0.5 * math.log2(math.e)

    half = QK_ROPE // 2
    w_uqT = w_uq.T.astype(BF16)
    wdkv_c = w_dkv[:, :KV_LORA].astype(BF16)
    k1, k2 = w_dkv[:, KV_LORA:KV_LORA + half], w_dkv[:, KV_LORA + half:]
    w_kr = jnp.concatenate([k1, k2, k2, k1], axis=-1).astype(BF16)
    wkv = w_ukv.reshape(KV_LORA, MLA_HEADS, QK_NOPE + V_HEAD)
    w_uk = wkv[..., :QK_NOPE].reshape(KV_LORA, MLA_HEADS * QK_NOPE).astype(BF16)
    w_uvT = wkv[..., QK_NOPE:].reshape(KV_LORA, MLA_HEADS * V_HEAD).T.astype(BF16)

    inv = ROPE_THETA ** (-jnp.arange(0, QK_ROPE, 2, dtype=F32) / QK_ROPE)
    ang = jnp.arange(s, dtype=F32)[:, None] * inv[None, :]
    cos, sin = jnp.cos(ang), jnp.sin(ang)
    zero = jnp.zeros_like(cos)
    t0 = jnp.concatenate([cos, cos, zero, zero], axis=-1)
    t1 = jnp.concatenate([-sin, sin, zero, zero], axis=-1)

    grid = (b, s // tm)
    row = lambda bi, i: (bi, i, 0)
    col = lambda bi, i: (bi, 0, i)
    tab = pl.BlockSpec((tm, LANES), lambda bi, i: (i, 0))
    tabT = pl.BlockSpec((half, tm), lambda bi, i: (0, i))
    return pl.pallas_call(
        functools.partial(_mla_proj_kernel, q_scale=q_scale),
        grid=grid,
        in_specs=[
            pl.BlockSpec((1, tm, D_MODEL), row),
            _const_spec((1, D_MODEL)),
            _const_spec((D_MODEL, Q_LORA)),
            _const_spec((1, Q_LORA)),
            _const_spec((MLA_HEADS * (QK_NOPE + QK_ROPE), Q_LORA)),
            _const_spec((D_MODEL, KV_LORA)),
            _const_spec((D_MODEL, LANES)),
            _const_spec((1, KV_LORA)),
            _const_spec((KV_LORA, MLA_HEADS * QK_NOPE)),
            _const_spec((MLA_HEADS * V_HEAD, KV_LORA)),
            tabT, tabT, tab, tab,
        ],
        out_specs=[
            pl.BlockSpec((1, MLA_HEADS * QK_PAD, tm), col),
            pl.BlockSpec((1, tm, MLA_HEADS * QK_PAD), row),
            pl.BlockSpec((1, MLA_HEADS * VT_ROWS, tm), col),
        ],
        out_shape=[
            jax.ShapeDtypeStruct((b, MLA_HEADS * QK_PAD, s), BF16),
            jax.ShapeDtypeStruct((b, s, MLA_HEADS * QK_PAD), BF16),
            jax.ShapeDtypeStruct((b, MLA_HEADS * VT_ROWS, s), BF16),
        ],
        compiler_params=_params(2),
        name="mla_proj",
    )(x, g.reshape(1, -1), w_dq.astype(BF16), q_norm.reshape(1, -1), w_uqT, wdkv_c, w_kr,
      kv_norm.reshape(1, -1), w_uk, w_uvT, cos.T, sin.T, t0, t1)


def _mla_attn_kernel(qT_ref, k_ref, vT_ref, o_ref, sa_scr, sb_scr, m_scr, acc_scr, *, tk, chunks_per_trip):
    nk = k_ref.shape[1] // tk
    tq = qT_ref.shape[2]
    m_scr[...] = jnp.full(m_scr.shape, MASK_VALUE, F32)
    acc_scr[...] = jnp.zeros(acc_scr.shape, F32)
    s_bufs = (sa_scr, sb_scr)
    groups = [slice(c, c + MXU_WIDTH) for c in range(0, tq, MXU_WIDTH)]

    def scores(j, dst, g):
        off = pl.multiple_of(j * tk, tk)
        dst[:, g] = jnp.dot(k_ref[0, pl.ds(off, tk), :], qT_ref[0, :, g], preferred_element_type=F32)

    def softmax_pv(src, j, g):
        off = pl.multiple_of(j * tk, tk)
        m_prev = m_scr[:, g]
        m_next = jnp.maximum(m_prev, jnp.max(src[:, g], axis=0, keepdims=True))
        alpha = jnp.exp2(m_prev - m_next)
        p = jnp.exp2(src[:, g] - m_next).astype(BF16)
        pv = jnp.dot(vT_ref[0, :, pl.ds(off, tk)], p, preferred_element_type=F32)
        acc_scr[:, g] = alpha * acc_scr[:, g] + pv
        m_scr[:, g] = m_next

    def chunk(j, parity, last):
        for g in groups:
            if not last:
                scores(j + 1, s_bufs[1 - parity], g)
            softmax_pv(s_bufs[parity], j, g)

    for g in groups:
        scores(0, sa_scr, g)
    n_trips = (nk - 1) // chunks_per_trip

    def body(t, carry):
        for k in range(chunks_per_trip):
            chunk(t * chunks_per_trip + k, k % 2, False)
        return carry

    if n_trips > 0:
        lax.fori_loop(0, n_trips, body, 0)
    for j in range(n_trips * chunks_per_trip, nk):
        chunk(j, j % 2, j == nk - 1)
    oT = acc_scr[:V_HEAD, :] / acc_scr[V_HEAD:V_HEAD + 1, :]
    o_ref[0] = oT.T.astype(BF16)


def _mla_attn(qT, k, vT, *, tq, tk, chunks_per_trip):
    b, s, _ = k.shape
    assert s % tk == 0 and s % tq == 0 and chunks_per_trip % 2 == 0
    grid = (b, MLA_HEADS, s // tq)
    return pl.pallas_call(
        functools.partial(_mla_attn_kernel, tk=tk, chunks_per_trip=chunks_per_trip),
        grid=grid,
        in_specs=[
            pl.BlockSpec((1, QK_PAD, tq), lambda bi, h, i: (bi, h, i)),
            pl.BlockSpec((1, s, QK_PAD), lambda bi, h, i: (bi, 0, h)),
            pl.BlockSpec((1, VT_ROWS, s), lambda bi, h, i: (bi, h, 0)),
        ],
        out_specs=pl.BlockSpec((1, tq, V_HEAD), lambda bi, h, i: (bi, i, h)),
        out_shape=jax.ShapeDtypeStruct((b, s, MLA_HEADS * V_HEAD), BF16),
        scratch_shapes=[pltpu.VMEM((tk, tq), F32), pltpu.VMEM((tk, tq), F32),
                        pltpu.VMEM((1, tq), F32), pltpu.VMEM((VT_ROWS, tq), F32)],
        compiler_params=_params(3),
        name="mla_attn",
    )(qT, k, vT)


def _na_proj_kernel(x_ref, g_ref, w_ref, o_ref, *, q_scale):
    h = _rms(x_ref[0], g_ref[...]).astype(BF16)
    qkv = jnp.dot(h, w_ref[...], preferred_element_type=F32)
    o_ref[0, :, :D_MODEL] = (qkv[:, :D_MODEL] * q_scale).astype(BF16)
    o_ref[0, :, D_MODEL:] = qkv[:, D_MODEL:].astype(BF16)


def _na_proj(x, g, w_qkv, *, tm):
    b, s, _ = x.shape
    row = lambda bi, i: (bi, i, 0)
    return pl.pallas_call(
        functools.partial(_na_proj_kernel, q_scale=NA_HEAD_DIM ** -0.5),
        grid=(b, s // tm),
        in_specs=[pl.BlockSpec((1, tm, D_MODEL), row), _const_spec((1, D_MODEL)),
                  _const_spec((D_MODEL, 3 * D_MODEL))],
        out_specs=pl.BlockSpec((1, tm, 3 * D_MODEL), row),
        out_shape=jax.ShapeDtypeStruct((b, s, 3 * D_MODEL), BF16),
        compiler_params=_params(2),
        name="na_proj",
    )(x, g.reshape(1, -1), w_qkv.astype(BF16))


def _na_bias_table(rpb):
    c = np.arange(GRID_W)[:, None]
    kc = np.arange(GRID_W)[None, :]
    c0 = np.clip(c - NA_KW // 2, 0, GRID_W - NA_KW)
    valid = (kc >= c0) & (kc < c0 + NA_KW)
    rel_col = kc - c + NA_KW - 1
    n_rel = 2 * NA_KW - 1
    onehot = (valid[..., None] & (rel_col[..., None] == np.arange(n_rel))).astype(np.float32)
    band = jnp.einsum("hrm,ckm->hrck", rpb.astype(F32), jnp.asarray(onehot),
                      precision=lax.Precision.HIGHEST)
    band = jnp.where(jnp.asarray(valid), band, MASK_VALUE)
    tbl = jnp.stack([band[:, NA_KH - 1 - pat:2 * NA_KH - 1 - pat] for pat in range(NA_KH)])
    tbl = tbl.transpose(0, 1, 3, 2, 4)
    return tbl.reshape(NA_KH, NA_PAIRS, 2 * GRID_W, NA_WIN)


def _na_attn_kernel(q_ref, k_ref, v_ref, bias_ref, o_ref, *, rows_per_step, n_rows):
    rb = pl.program_id(2)
    lane = lax.broadcasted_iota(jnp.int32, (GRID_W, LANES), 1)
    first = lane < NA_HEAD_DIM

    def window(i):
        r = rb * rows_per_step + i
        r0 = jnp.clip(r - NA_KH // 2, 0, n_rows - NA_KH)
        return r - r0, pl.multiple_of(r0 * GRID_W, GRID_W)

    def scores(i):
        pat, koff = window(i)
        q2 = q_ref[0, i * GRID_W:(i + 1) * GRID_W, :]
        zero = jnp.zeros_like(q2)
        qs = jnp.concatenate([jnp.where(first, q2, zero), jnp.where(first, zero, q2)], axis=0)
        kw = k_ref[0, pl.ds(koff, NA_WIN), :]
        s = lax.dot_general(qs, kw, (((1,), (1,)), ((), ())), preferred_element_type=F32)
        return s + bias_ref[pat, 0]

    def finish(i, s):
        _, koff = window(i)
        vw = v_ref[0, pl.ds(koff, NA_WIN), :]
        m = jnp.max(s, axis=1, keepdims=True)
        p = jnp.exp(s - m)
        l = jnp.sum(p, axis=1, keepdims=True)
        pv = jnp.dot(p.astype(BF16), vw, preferred_element_type=F32) / l
        o = jnp.where(first, pv[:GRID_W], pv[GRID_W:])
        o_ref[0, i * GRID_W:(i + 1) * GRID_W, :] = o.astype(BF16)

    ahead = min(3, rows_per_step)
    pending = [scores(i) for i in range(ahead)]
    for i in range(rows_per_step):
        if i + ahead < rows_per_step:
            pending.append(scores(i + ahead))
        finish(i, pending[i])
        pending[i] = None


def _na_attn(qkv, bias, *, rows_per_step):
    b, s, _ = qkv.shape
    n_rows = s // GRID_W
    tq = rows_per_step * GRID_W
    grid = (NA_PAIRS, b, n_rows // rows_per_step)
    return pl.pallas_call(
        functools.partial(_na_attn_kernel, rows_per_step=rows_per_step, n_rows=n_rows),
        grid=grid,
        in_specs=[
            pl.BlockSpec((1, tq, LANES), lambda p, bi, i: (bi, i, p)),
            pl.BlockSpec((1, s, LANES), lambda p, bi, i: (bi, 0, NA_PAIRS + p)),
            pl.BlockSpec((1, s, LANES), lambda p, bi, i: (bi, 0, 2 * NA_PAIRS + p)),
            pl.BlockSpec((NA_KH, 1, 2 * GRID_W, NA_WIN), lambda p, bi, i: (0, p, 0, 0)),
        ],
        out_specs=pl.BlockSpec((1, tq, LANES), lambda p, bi, i: (bi, i, p)),
        out_shape=jax.ShapeDtypeStruct((b, s, D_MODEL), BF16),
        compiler_params=_params(3),
        name="na_attn",
    )(qkv, qkv, qkv, bias)


def _post_kernel(o_ref, x_ref, wo_ref, g_ref, w1_ref, w2_ref, gf_ref, y_ref, *, ff_chunk, final):
    x = x_ref[0] + jnp.dot(o_ref[0], wo_ref[...], preferred_element_type=F32)
    h = _rms(x, g_ref[...]).astype(BF16)
    y = x
    for c in range(D_FF // ff_chunk):
        a = jnp.dot(h, w1_ref[:, c * ff_chunk:(c + 1) * ff_chunk], preferred_element_type=F32)
        a = jnp.maximum(a, 0.0)
        y = y + jnp.dot((a * a).astype(BF16), w2_ref[c * ff_chunk:(c + 1) * ff_chunk, :],
                        preferred_element_type=F32)
    if final:
        y = _rms(y, gf_ref[...])
    y_ref[0] = y


def _post(o, x, w_o, g_mlp, w1, w2, g_final, *, tm, final):
    b, s, d_o = o.shape
    row = lambda bi, i: (bi, i, 0)
    return pl.pallas_call(
        functools.partial(_post_kernel, ff_chunk=1024, final=final),
        grid=(b, s // tm),
        in_specs=[
            pl.BlockSpec((1, tm, d_o), row),
            pl.BlockSpec((1, tm, D_MODEL), row),
            _const_spec((d_o, D_MODEL)),
            _const_spec((1, D_MODEL)),
            _const_spec((D_MODEL, D_FF)),
            _const_spec((D_FF, D_MODEL)),
            _const_spec((1, D_MODEL)),
        ],
        out_specs=pl.BlockSpec((1, tm, D_MODEL), row),
        out_shape=jax.ShapeDtypeStruct((b, s, D_MODEL), F32),
        compiler_params=_params(2),
        name="post_final" if final else "post",
    )(o, x, w_o.astype(BF16), g_mlp.reshape(1, -1), w1.astype(BF16), w2.astype(BF16),
      g_final.reshape(1, -1))


def _trunk(x, attn_norm, mlp_norm, final_norm, mla_w_dq, mla_q_norm, mla_w_uq, mla_w_dkv,
           mla_kv_norm, mla_w_ukv, mla_w_o, na_w_qkv, na_bias, na_w_o, mlp_w1, mlp_w2):
    s = x.shape[1]
    tm = min(512, s)
    qT, k, vT = _mla_proj(x, attn_norm[0], mla_w_dq[0], mla_q_norm[0], mla_w_uq[0], mla_w_dkv[0],
                          mla_kv_norm[0], mla_w_ukv[0], tm=tm)
    tq, tk = (1024, 1024) if s >= 8192 else (min(2048, s), min(512, s // 2))
    o = _mla_attn(qT, k, vT, tq=tq, tk=tk, chunks_per_trip=4 if s >= 8192 else 2)
    x = _post(o, x, mla_w_o[0], mlp_norm[0], mlp_w1[0], mlp_w2[0], final_norm, tm=tm, final=False)
    qkv = _na_proj(x, attn_norm[1], na_w_qkv[0], tm=tm)
    o = _na_attn(qkv, na_bias, rows_per_step=min(64, s // GRID_W))
    return _post(o, x, na_w_o[0], mlp_norm[1], mlp_w1[1], mlp_w2[1], final_norm, tm=tm, final=True)


def kernel(x_prompt, x_sample, attn_norm, mlp_norm, final_norm, mla_w_dq, mla_q_norm, mla_w_uq, mla_w_dkv, mla_kv_norm, mla_w_ukv, mla_w_o, na_w_qkv, na_rpb, na_w_o, mlp_w1, mlp_w2):
    na_bias = _na_bias_table(na_rpb[0])
    args = (attn_norm, mlp_norm, final_norm, mla_w_dq, mla_q_norm, mla_w_uq, mla_w_dkv, mla_kv_norm,
            mla_w_ukv, mla_w_o, na_w_qkv, na_bias, na_w_o, mlp_w1, mlp_w2)
    return (_trunk(x_prompt, *args), _trunk(x_sample, *args))
```

```python
import functools
import math

import numpy as np
import jax
import jax.numpy as jnp
from jax import lax
from jax.experimental import pallas as pl
from jax.experimental.pallas import tpu as pltpu

D_MODEL = 1024
GRID_W = 64
MLA_HEADS = 16
Q_LORA = 384
KV_LORA = 256
QK_NOPE = 128
QK_ROPE = 64
V_HEAD = 128
ROPE_THETA = 10000.0
NA_HEADS = 16
NA_HEAD_DIM = D_MODEL // NA_HEADS
NA_KH = 8
NA_KW = 16
D_FF = 4 * D_MODEL
EPS = 1e-6

LANES = 128
MXU_WIDTH = 256
BF16_ROWS = 16
QK_PAD = 2 * LANES
VT_ROWS = V_HEAD + BF16_ROWS
NA_PAIRS = NA_HEADS * NA_HEAD_DIM // LANES
NA_WIN = NA_KH * GRID_W
MASK_VALUE = -1e30
VMEM_LIMIT = 56 * 1024 * 1024

F32 = jnp.float32
BF16 = jnp.bfloat16


def _rms(x, g):
    return x * lax.rsqrt(jnp.mean(x * x, axis=-1, keepdims=True) + EPS) * g


def _const_spec(shape):
    zeros = (0,) * len(shape)
    return pl.BlockSpec(shape, lambda *_: zeros, pipeline_mode=pl.Buffered(1))


def _params(n_axes):
    return pltpu.CompilerParams(
        dimension_semantics=("arbitrary",) * n_axes, vmem_limit_bytes=VMEM_LIMIT)


def _mla_proj_kernel(x_ref, g_ref, wdq_ref, qn_ref, wuqT_ref, wdkvc_ref, wkr_ref, kvn_ref,
                     wuk_ref, wuvT_ref, cosT_ref, sinT_ref, tk0_ref, tk1_ref,
                     qT_ref, k_ref, vT_ref, *, q_scale):
    nt = (((1,), (1,)), ((), ()))
    half = QK_ROPE // 2
    d_qk = QK_NOPE + QK_ROPE
    h = _rms(x_ref[0], g_ref[...]).astype(BF16)
    tm = h.shape[0]
    cq = jnp.dot(h, wdq_ref[...], preferred_element_type=F32)
    cq = _rms(cq, qn_ref[...]).astype(BF16)
    qT = lax.dot_general(wuqT_ref[...], cq, nt, preferred_element_type=F32)
    cos = cosT_ref[...] * q_scale
    sin = sinT_ref[...] * q_scale
    zpad = jnp.zeros((QK_PAD - d_qk, tm), BF16)
    for hd in range(MLA_HEADS):
        src, dst = hd * d_qk, hd * QK_PAD
        x1 = qT[src + QK_NOPE:src + QK_NOPE + half]
        x2 = qT[src + QK_NOPE + half:src + d_qk]
        qT_ref[0, dst:dst + QK_NOPE, :] = (qT[src:src + QK_NOPE] * q_scale).astype(BF16)
        qT_ref[0, dst + QK_NOPE:dst + QK_NOPE + half, :] = (x1 * cos - x2 * sin).astype(BF16)
        qT_ref[0, dst + QK_NOPE + half:dst + d_qk, :] = (x1 * sin + x2 * cos).astype(BF16)
        qT_ref[0, dst + d_qk:dst + QK_PAD, :] = zpad

    ckv = jnp.dot(h, wdkvc_ref[...], preferred_element_type=F32)
    ckv = _rms(ckv, kvn_ref[...]).astype(BF16)
    kn = jnp.dot(ckv, wuk_ref[...], preferred_element_type=F32).astype(BF16)
    kr = jnp.dot(h, wkr_ref[...], preferred_element_type=F32)
    kr = (kr * tk0_ref[...] + pltpu.roll(kr, LANES // 2, axis=1) * tk1_ref[...]).astype(BF16)
    for hd in range(MLA_HEADS):
        k_ref[0, :, hd * QK_PAD:hd * QK_PAD + QK_NOPE] = kn[:, hd * QK_NOPE:(hd + 1) * QK_NOPE]
        k_ref[0, :, hd * QK_PAD + QK_NOPE:(hd + 1) * QK_PAD] = kr
    vT = lax.dot_general(wuvT_ref[...], ckv, nt, preferred_element_type=F32).astype(BF16)
    ones_row = lax.broadcasted_iota(jnp.int32, (VT_ROWS - V_HEAD, tm), 0) == 0
    tail = jnp.where(ones_row, 1.0, 0.0).astype(BF16)
    for hd in range(MLA_HEADS):
        vT_ref[0, hd * VT_ROWS:hd * VT_ROWS + V_HEAD, :] = vT[hd * V_HEAD:(hd + 1) * V_HEAD]
        vT_ref[0, hd * VT_ROWS + V_HEAD:(hd + 1) * VT_ROWS, :] = tail


def _mla_proj(x, g, w_dq, q_norm, w_uq, w_dkv, kv_norm, w_ukv, *, tm):
    b, s, _ = x.shape
    q_scale = (QK_NOPE + QK_ROPE) ** -0.5 * math.log2(math.e)

    half = QK_ROPE // 2
    w_uqT = w_uq.T.astype(BF16)
    wdkv_c = w_dkv[:, :KV_LORA].astype(BF16)
    k1, k2 = w_dkv[:, KV_LORA:KV_LORA + half], w_dkv[:, KV_LORA + half:]
    w_kr = jnp.concatenate([k1, k2, k2, k1], axis=-1).astype(BF16)
    wkv = w_ukv.reshape(KV_LORA, MLA_HEADS, QK_NOPE + V_HEAD)
    w_uk = wkv[..., :QK_NOPE].reshape(KV_LORA, MLA_HEADS * QK_NOPE).astype(BF16)
    w_uvT = wkv[..., QK_NOPE:].reshape(KV_LORA, MLA_HEADS * V_HEAD).T.astype(BF16)

    inv = ROPE_THETA ** (-jnp.arange(0, QK_ROPE, 2, dtype=F32) / QK_ROPE)
    ang = jnp.arange(s, dtype=F32)[:, None] * inv[None, :]
    cos, sin = jnp.cos(ang), jnp.sin(ang)
    zero = jnp.zeros_like(cos)
    t0 = jnp.concatenate([cos, cos, zero, zero], axis=-1)
    t1 = jnp.concatenate([-sin, sin, zero, zero], axis=-1)

    grid = (b, s // tm)
    row = lambda bi, i: (bi, i, 0)
    col = lambda bi, i: (bi, 0, i)
    tab = pl.BlockSpec((tm, LANES), lambda bi, i: (i, 0))
    tabT = pl.BlockSpec((half, tm), lambda bi, i: (0, i))
    return pl.pallas_call(
        functools.partial(_mla_proj_kernel, q_scale=q_scale),
        grid=grid,
        in_specs=[
            pl.BlockSpec((1, tm, D_MODEL), row),
            _const_spec((1, D_MODEL)),
            _const_spec((D_MODEL, Q_LORA)),
            _const_spec((1, Q_LORA)),
            _const_spec((MLA_HEADS * (QK_NOPE + QK_ROPE), Q_LORA)),
            _const_spec((D_MODEL, KV_LORA)),
            _const_spec((D_MODEL, LANES)),
            _const_spec((1, KV_LORA)),
            _const_spec((KV_LORA, MLA_HEADS * QK_NOPE)),
            _const_spec((MLA_HEADS * V_HEAD, KV_LORA)),
            tabT, tabT, tab, tab,
        ],
        out_specs=[
            pl.BlockSpec((1, MLA_HEADS * QK_PAD, tm), col),
            pl.BlockSpec((1, tm, MLA_HEADS * QK_PAD), row),
            pl.BlockSpec((1, MLA_HEADS * VT_ROWS, tm), col),
        ],
        out_shape=[
            jax.ShapeDtypeStruct((b, MLA_HEADS * QK_PAD, s), BF16),
            jax.ShapeDtypeStruct((b, s, MLA_HEADS * QK_PAD), BF16),
            jax.ShapeDtypeStruct((b, MLA_HEADS * VT_ROWS, s), BF16),
        ],
        compiler_params=_params(2),
        name="mla_proj",
    )(x, g.reshape(1, -1), w_dq.astype(BF16), q_norm.reshape(1, -1), w_uqT, wdkv_c, w_kr,
      kv_norm.reshape(1, -1), w_uk, w_uvT, cos.T, sin.T, t0, t1)


def _mla_attn_kernel(qT_ref, k_ref, vT_ref, o_ref, sa_scr, sb_scr, ca_scr, cb_scr, m_scr, acc_scr, *, tk,
                     chunks_per_trip):
    nk = k_ref.shape[1] // tk
    tq = qT_ref.shape[2]
    m_scr[...] = jnp.full(m_scr.shape, MASK_VALUE, F32)
    acc_scr[...] = jnp.zeros(acc_scr.shape, F32)
    s_bufs = ((sa_scr, ca_scr), (sb_scr, cb_scr))
    groups = [slice(c, c + MXU_WIDTH) for c in range(0, tq, MXU_WIDTH)]

    def scores(j, bufs, g):
        dst, dst_max = bufs
        off = pl.multiple_of(j * tk, tk)
        s = jnp.dot(k_ref[0, pl.ds(off, tk), :], qT_ref[0, :, g], preferred_element_type=F32)
        dst[:, g] = s
        dst_max[:, g] = jnp.max(s, axis=0, keepdims=True)

    def softmax_pv(bufs, j, g):
        src, src_max = bufs
        off = pl.multiple_of(j * tk, tk)
        m_prev = m_scr[:, g]
        m_next = jnp.maximum(m_prev, src_max[:, g])
        alpha = jnp.exp2(m_prev - m_next)
        p = jnp.exp2(src[:, g] - m_next).astype(BF16)
        pv = jnp.dot(vT_ref[0, :, pl.ds(off, tk)], p, preferred_element_type=F32)
        acc_scr[:, g] = alpha * acc_scr[:, g] + pv
        m_scr[:, g] = m_next

    def chunk(j, parity, last):
        for g in groups:
            if not last:
                scores(j + 1, s_bufs[1 - parity], g)
            softmax_pv(s_bufs[parity], j, g)

    for g in groups:
        scores(0, s_bufs[0], g)
    n_trips = (nk - 1) // chunks_per_trip

    def body(t, carry):
        for k in range(chunks_per_trip):
            chunk(t * chunks_per_trip + k, k % 2, False)
        return carry

    if n_trips > 0:
        lax.fori_loop(0, n_trips, body, 0)
    for j in range(n_trips * chunks_per_trip, nk):
        chunk(j, j % 2, j == nk - 1)
    oT = acc_scr[:V_HEAD, :] / acc_scr[V_HEAD:V_HEAD + 1, :]
    o_ref[0] = oT.T.astype(BF16)


def _mla_attn(qT, k, vT, *, tq, tk, chunks_per_trip):
    b, s, _ = k.shape
    assert s % tk == 0 and s % tq == 0 and chunks_per_trip % 2 == 0
    grid = (b, MLA_HEADS, s // tq)
    return pl.pallas_call(
        functools.partial(_mla_attn_kernel, tk=tk, chunks_per_trip=chunks_per_trip),
        grid=grid,
        in_specs=[
            pl.BlockSpec((1, QK_PAD, tq), lambda bi, h, i: (bi, h, i)),
            pl.BlockSpec((1, s, QK_PAD), lambda bi, h, i: (bi, 0, h)),
            pl.BlockSpec((1, VT_ROWS, s), lambda bi, h, i: (bi, h, 0)),
        ],
        out_specs=pl.BlockSpec((1, tq, V_HEAD), lambda bi, h, i: (bi, i, h)),
        out_shape=jax.ShapeDtypeStruct((b, s, MLA_HEADS * V_HEAD), BF16),
        scratch_shapes=[pltpu.VMEM((tk, tq), F32), pltpu.VMEM((tk, tq), F32),
                        pltpu.VMEM((1, tq), F32), pltpu.VMEM((1, tq), F32),
                        pltpu.VMEM((1, tq), F32), pltpu.VMEM((VT_ROWS, tq), F32)],
        compiler_params=_params(3),
        name="mla_attn",
    )(qT, k, vT)


def _na_proj_kernel(x_ref, g_ref, w_ref, o_ref, *, q_scale):
    h = _rms(x_ref[0], g_ref[...]).astype(BF16)
    qkv = jnp.dot(h, w_ref[...], preferred_element_type=F32)
    o_ref[0, :, :D_MODEL] = (qkv[:, :D_MODEL] * q_scale).astype(BF16)
    o_ref[0, :, D_MODEL:] = qkv[:, D_MODEL:].astype(BF16)


def _na_proj(x, g, w_qkv, *, tm):
    b, s, _ = x.shape
    row = lambda bi, i: (bi, i, 0)
    return pl.pallas_call(
        functools.partial(_na_proj_kernel, q_scale=NA_HEAD_DIM ** -0.5),
        grid=(b, s // tm),
        in_specs=[pl.BlockSpec((1, tm, D_MODEL), row), _const_spec((1, D_MODEL)),
                  _const_spec((D_MODEL, 3 * D_MODEL))],
        out_specs=pl.BlockSpec((1, tm, 3 * D_MODEL), row),
        out_shape=jax.ShapeDtypeStruct((b, s, 3 * D_MODEL), BF16),
        compiler_params=_params(2),
        name="na_proj",
    )(x, g.reshape(1, -1), w_qkv.astype(BF16))


def _na_bias_table(rpb):
    c = np.arange(GRID_W)[:, None]
    kc = np.arange(GRID_W)[None, :]
    c0 = np.clip(c - NA_KW // 2, 0, GRID_W - NA_KW)
    valid = (kc >= c0) & (kc < c0 + NA_KW)
    rel_col = kc - c + NA_KW - 1
    n_rel = 2 * NA_KW - 1
    onehot = (valid[..., None] & (rel_col[..., None] == np.arange(n_rel))).astype(np.float32)
    band = jnp.einsum("hrm,ckm->hrck", rpb.astype(F32), jnp.asarray(onehot),
                      precision=lax.Precision.HIGHEST)
    band = jnp.where(jnp.asarray(valid), band, MASK_VALUE)
    tbl = jnp.stack([band[:, NA_KH - 1 - pat:2 * NA_KH - 1 - pat] for pat in range(NA_KH)])
    tbl = tbl.transpose(0, 1, 3, 2, 4)
    return tbl.reshape(NA_KH, NA_PAIRS, 2 * GRID_W, NA_WIN)


def _na_attn_kernel(q_ref, k_ref, v_ref, bias_ref, o_ref, *, rows_per_step, n_rows):
    rb = pl.program_id(2)
    lane = lax.broadcasted_iota(jnp.int32, (GRID_W, LANES), 1)
    first = lane < NA_HEAD_DIM

    def window(i):
        r = rb * rows_per_step + i
        r0 = jnp.clip(r - NA_KH // 2, 0, n_rows - NA_KH)
        return r - r0, pl.multiple_of(r0 * GRID_W, GRID_W)

    def scores(i):
        pat, koff = window(i)
        q2 = q_ref[0, i * GRID_W:(i + 1) * GRID_W, :]
        zero = jnp.zeros_like(q2)
        qs = jnp.concatenate([jnp.where(first, q2, zero), jnp.where(first, zero, q2)], axis=0)
        kw = k_ref[0, pl.ds(koff, NA_WIN), :]
        s = lax.dot_general(qs, kw, (((1,), (1,)), ((), ())), preferred_element_type=F32)
        return s + bias_ref[pat, 0]

    def finish(i, s):
        _, koff = window(i)
        vw = v_ref[0, pl.ds(koff, NA_WIN), :]
        m = jnp.max(s, axis=1, keepdims=True)
        p = jnp.exp(s - m)
        l = jnp.sum(p, axis=1, keepdims=True)
        pv = jnp.dot(p.astype(BF16), vw, preferred_element_type=F32) / l
        o = jnp.where(first, pv[:GRID_W], pv[GRID_W:])
        o_ref[0, i * GRID_W:(i + 1) * GRID_W, :] = o.astype(BF16)

    ahead = min(3, rows_per_step)
    pending = [scores(i) for i in range(ahead)]
    for i in range(rows_per_step):
        if i + ahead < rows_per_step:
            pending.append(scores(i + ahead))
        finish(i, pending[i])
        pending[i] = None


def _na_attn(qkv, bias, *, rows_per_step):
    b, s, _ = qkv.shape
    n_rows = s // GRID_W
    tq = rows_per_step * GRID_W
    grid = (NA_PAIRS, b, n_rows // rows_per_step)
    return pl.pallas_call(
        functools.partial(_na_attn_kernel, rows_per_step=rows_per_step, n_rows=n_rows),
        grid=grid,
        in_specs=[
            pl.BlockSpec((1, tq, LANES), lambda p, bi, i: (bi, i, p)),
            pl.BlockSpec((1, s, LANES), lambda p, bi, i: (bi, 0, NA_PAIRS + p)),
            pl.BlockSpec((1, s, LANES), lambda p, bi, i: (bi, 0, 2 * NA_PAIRS + p)),
            pl.BlockSpec((NA_KH, 1, 2 * GRID_W, NA_WIN), lambda p, bi, i: (0, p, 0, 0)),
        ],
        out_specs=pl.BlockSpec((1, tq, LANES), lambda p, bi, i: (bi, i, p)),
        out_shape=jax.ShapeDtypeStruct((b, s, D_MODEL), BF16),
        compiler_params=_params(3),
        name="na_attn",
    )(qkv, qkv, qkv, bias)


def _post_kernel(o_ref, x_ref, wo_ref, g_ref, w1_ref, w2_ref, gf_ref, y_ref, *, ff_chunk, final):
    x = x_ref[0] + jnp.dot(o_ref[0], wo_ref[...], preferred_element_type=F32)
    h = _rms(x, g_ref[...]).astype(BF16)
    y = x
    for c in range(D_FF // ff_chunk):
        a = jnp.dot(h, w1_ref[:, c * ff_chunk:(c + 1) * ff_chunk], preferred_element_type=F32)
        a = jnp.maximum(a, 0.0)
        y = y + jnp.dot((a * a).astype(BF16), w2_ref[c * ff_chunk:(c + 1) * ff_chunk, :],
                        preferred_element_type=F32)
    if final:
        y = _rms(y, gf_ref[...])
    y_ref[0] = y


def _post(o, x, w_o, g_mlp, w1, w2, g_final, *, tm, final):
    b, s, d_o = o.shape
    row = lambda bi, i: (bi, i, 0)
    return pl.pallas_call(
        functools.partial(_post_kernel, ff_chunk=1024, final=final),
        grid=(b, s // tm),
        in_specs=[
            pl.BlockSpec((1, tm, d_o), row),
            pl.BlockSpec((1, tm, D_MODEL), row),
            _const_spec((d_o, D_MODEL)),
            _const_spec((1, D_MODEL)),
            _const_spec((D_MODEL, D_FF)),
            _const_spec((D_FF, D_MODEL)),
            _const_spec((1, D_MODEL)),
        ],
        out_specs=pl.BlockSpec((1, tm, D_MODEL), row),
        out_shape=jax.ShapeDtypeStruct((b, s, D_MODEL), F32),
        compiler_params=_params(2),
        name="post_final" if final else "post",
    )(o, x, w_o.astype(BF16), g_mlp.reshape(1, -1), w1.astype(BF16), w2.astype(BF16),
      g_final.reshape(1, -1))


def _trunk(x, attn_norm, mlp_norm, final_norm, mla_w_dq, mla_q_norm, mla_w_uq, mla_w_dkv,
           mla_kv_norm, mla_w_ukv, mla_w_o, na_w_qkv, na_bias, na_w_o, mlp_w1, mlp_w2):
    s = x.shape[1]
    tm = min(512, s)
    qT, k, vT = _mla_proj(x, attn_norm[0], mla_w_dq[0], mla_q_norm[0], mla_w_uq[0], mla_w_dkv[0],
                          mla_kv_norm[0], mla_w_ukv[0], tm=tm)
    tq, tk = (1024, 1024) if s >= 8192 else (min(2048, s), min(512, s // 2))
    o = _mla_attn(qT, k, vT, tq=tq, tk=tk, chunks_per_trip=4 if s >= 8192 else 2)
    x = _post(o, x, mla_w_o[0], mlp_norm[0], mlp_w1[0], mlp_w2[0], final_norm, tm=tm, final=False)
    qkv = _na_proj(x, attn_norm[1], na_w_qkv[0], tm=tm)
    o = _na_attn(qkv, na_bias, rows_per_step=min(64, s // GRID_W))
    return _post(o, x, na_w_o[0], mlp_norm[1], mlp_w1[1], mlp_w2[1], final_norm, tm=tm, final=True)


def kernel(x_prompt, x_sample, attn_norm, mlp_norm, final_norm, mla_w_dq, mla_q_norm, mla_w_uq, mla_w_dkv, mla_kv_norm, mla_w_ukv, mla_w_o, na_w_qkv, na_rpb, na_w_o, mlp_w1, mlp_w2):
    na_bias = _na_bias_table(na_rpb[0])
    args = (attn_norm, mlp_norm, final_norm, mla_w_dq, mla_q_norm, mla_w_uq, mla_w_dkv, mla_kv_norm,
            mla_w_ukv, mla_w_o, na_w_qkv, na_bias, na_w_o, mlp_w1, mlp_w2)
    return (_trunk(x_prompt, *args), _trunk(x_sample, *args))
```

```python
import functools
import math

import numpy as np
import jax
import jax.numpy as jnp
from jax import lax
from jax.experimental import pallas as pl
from jax.experimental.pallas import tpu as pltpu

D_MODEL = 1024
GRID_W = 64
MLA_HEADS = 16
Q_LORA = 384
KV_LORA = 256
QK_NOPE = 128
QK_ROPE = 64
V_HEAD = 128
ROPE_THETA = 10000.0
NA_HEADS = 16
NA_HEAD_DIM = D_MODEL // NA_HEADS
NA_KH = 8
NA_KW = 16
D_FF = 4 * D_MODEL
EPS = 1e-6

LANES = 128
MXU_WIDTH = 256
BF16_ROWS = 16
QK_PAD = 2 * LANES
VT_ROWS = V_HEAD + BF16_ROWS
NA_PAIRS = NA_HEADS * NA_HEAD_DIM // LANES
NA_WIN = NA_KH * GRID_W
MASK_VALUE = -1e30
VMEM_LIMIT = 56 * 1024 * 1024

F32 = jnp.float32
BF16 = jnp.bfloat16


def _rms(x, g):
    return x * lax.rsqrt(jnp.mean(x * x, axis=-1, keepdims=True) + EPS) * g


def _const_spec(shape):
    zeros = (0,) * len(shape)
    return pl.BlockSpec(shape, lambda *_: zeros, pipeline_mode=pl.Buffered(1))


def _params(n_axes):
    return pltpu.CompilerParams(
        dimension_semantics=("arbitrary",) * n_axes, vmem_limit_bytes=VMEM_LIMIT)


def _mla_proj_kernel(x_ref, g_ref, wdq_ref, qn_ref, wuqT_ref, wdkvc_ref, wkr_ref, kvn_ref,
                     wuk_ref, wuvT_ref, cosT_ref, sinT_ref, tk0_ref, tk1_ref,
                     qT_ref, k_ref, vT_ref, *, q_scale):
    nt = (((1,), (1,)), ((), ()))
    half = QK_ROPE // 2
    d_qk = QK_NOPE + QK_ROPE
    h = _rms(x_ref[0], g_ref[...]).astype(BF16)
    tm = h.shape[0]
    cq = jnp.dot(h, wdq_ref[...], preferred_element_type=F32)
    cq = _rms(cq, qn_ref[...]).astype(BF16)
    qT = lax.dot_general(wuqT_ref[...], cq, nt, preferred_element_type=F32)
    cos = cosT_ref[...] * q_scale
    sin = sinT_ref[...] * q_scale
    zpad = jnp.zeros((QK_PAD - d_qk, tm), BF16)
    for hd in range(MLA_HEADS):
        src, dst = hd * d_qk, hd * QK_PAD
        x1 = qT[src + QK_NOPE:src + QK_NOPE + half]
        x2 = qT[src + QK_NOPE + half:src + d_qk]
        qT_ref[0, dst:dst + QK_NOPE, :] = (qT[src:src + QK_NOPE] * q_scale).astype(BF16)
        qT_ref[0, dst + QK_NOPE:dst + QK_NOPE + half, :] = (x1 * cos - x2 * sin).astype(BF16)
        qT_ref[0, dst + QK_NOPE + half:dst + d_qk, :] = (x1 * sin + x2 * cos).astype(BF16)
        qT_ref[0, dst + d_qk:dst + QK_PAD, :] = zpad

    ckv = jnp.dot(h, wdkvc_ref[...], preferred_element_type=F32)
    ckv = _rms(ckv, kvn_ref[...]).astype(BF16)
    kn = jnp.dot(ckv, wuk_ref[...], preferred_element_type=F32).astype(BF16)
    kr = jnp.dot(h, wkr_ref[...], preferred_element_type=F32)
    kr = (kr * tk0_ref[...] + pltpu.roll(kr, LANES // 2, axis=1) * tk1_ref[...]).astype(BF16)
    for hd in range(MLA_HEADS):
        k_ref[0, :, hd * QK_PAD:hd * QK_PAD + QK_NOPE] = kn[:, hd * QK_NOPE:(hd + 1) * QK_NOPE]
        k_ref[0, :, hd * QK_PAD + QK_NOPE:(hd + 1) * QK_PAD] = kr
    vT = lax.dot_general(wuvT_ref[...], ckv, nt, preferred_element_type=F32).astype(BF16)
    ones_row = lax.broadcasted_iota(jnp.int32, (VT_ROWS - V_HEAD, tm), 0) == 0
    tail = jnp.where(ones_row, 1.0, 0.0).astype(BF16)
    for hd in range(MLA_HEADS):
        vT_ref[0, hd * VT_ROWS:hd * VT_ROWS + V_HEAD, :] = vT[hd * V_HEAD:(hd + 1) * V_HEAD]
        vT_ref[0, hd * VT_ROWS + V_HEAD:(hd + 1) * VT_ROWS, :] = tail


def _mla_proj(x, g, w_dq, q_norm, w_uq, w_dkv, kv_norm, w_ukv, *, tm):
    b, s, _ = x.shape
    q_scale = (QK_NOPE + QK_ROPE) ** -0.5 * math.log2(math.e)

    half = QK_ROPE // 2
    w_uqT = w_uq.T.astype(BF16)
    wdkv_c = w_dkv[:, :KV_LORA].astype(BF16)
    k1, k2 = w_dkv[:, KV_LORA:KV_LORA + half], w_dkv[:, KV_LORA + half:]
    w_kr = jnp.concatenate([k1, k2, k2, k1], axis=-1).astype(BF16)
    wkv = w_ukv.reshape(KV_LORA, MLA_HEADS, QK_NOPE + V_HEAD)
    w_uk = wkv[..., :QK_NOPE].reshape(KV_LORA, MLA_HEADS * QK_NOPE).astype(BF16)
    w_uvT = wkv[..., QK_NOPE:].reshape(KV_LORA, MLA_HEADS * V_HEAD).T.astype(BF16)

    inv = ROPE_THETA ** (-jnp.arange(0, QK_ROPE, 2, dtype=F32) / QK_ROPE)
    ang = jnp.arange(s, dtype=F32)[:, None] * inv[None, :]
    cos, sin = jnp.cos(ang), jnp.sin(ang)
    zero = jnp.zeros_like(cos)
    t0 = jnp.concatenate([cos, cos, zero, zero], axis=-1)
    t1 = jnp.concatenate([-sin, sin, zero, zero], axis=-1)

    grid = (b, s // tm)
    row = lambda bi, i: (bi, i, 0)
    col = lambda bi, i: (bi, 0, i)
    tab = pl.BlockSpec((tm, LANES), lambda bi, i: (i, 0))
    tabT = pl.BlockSpec((half, tm), lambda bi, i: (0, i))
    return pl.pallas_call(
        functools.partial(_mla_proj_kernel, q_scale=q_scale),
        grid=grid,
        in_specs=[
            pl.BlockSpec((1, tm, D_MODEL), row),
            _const_spec((1, D_MODEL)),
            _const_spec((D_MODEL, Q_LORA)),
            _const_spec((1, Q_LORA)),
            _const_spec((MLA_HEADS * (QK_NOPE + QK_ROPE), Q_LORA)),
            _const_spec((D_MODEL, KV_LORA)),
            _const_spec((D_MODEL, LANES)),
            _const_spec((1, KV_LORA)),
            _const_spec((KV_LORA, MLA_HEADS * QK_NOPE)),
            _const_spec((MLA_HEADS * V_HEAD, KV_LORA)),
            tabT, tabT, tab, tab,
        ],
        out_specs=[
            pl.BlockSpec((1, MLA_HEADS * QK_PAD, tm), col),
            pl.BlockSpec((1, tm, MLA_HEADS * QK_PAD), row),
            pl.BlockSpec((1, MLA_HEADS * VT_ROWS, tm), col),
        ],
        out_shape=[
            jax.ShapeDtypeStruct((b, MLA_HEADS * QK_PAD, s), BF16),
            jax.ShapeDtypeStruct((b, s, MLA_HEADS * QK_PAD), BF16),
            jax.ShapeDtypeStruct((b, MLA_HEADS * VT_ROWS, s), BF16),
        ],
        compiler_params=_params(2),
        name="mla_proj",
    )(x, g.reshape(1, -1), w_dq.astype(BF16), q_norm.reshape(1, -1), w_uqT, wdkv_c, w_kr,
      kv_norm.reshape(1, -1), w_uk, w_uvT, cos.T, sin.T, t0, t1)


def _mla_attn_kernel(qT_ref, k_ref, vT_ref, o_ref, sa_scr, sb_scr, ca_scr, cb_scr, m_scr, acc_scr, *, tk,
                     chunks_per_trip):
    nk = k_ref.shape[1] // tk
    tq = qT_ref.shape[2]
    m_scr[...] = jnp.full(m_scr.shape, MASK_VALUE, F32)
    acc_scr[...] = jnp.zeros(acc_scr.shape, F32)
    s_bufs = ((sa_scr, ca_scr), (sb_scr, cb_scr))
    groups = [slice(c, c + MXU_WIDTH) for c in range(0, tq, MXU_WIDTH)]

    def scores(j, bufs, g):
        dst, dst_max = bufs
        off = pl.multiple_of(j * tk, tk)
        s = jnp.dot(k_ref[0, pl.ds(off, tk), :], qT_ref[0, :, g], preferred_element_type=F32)
        dst[:, g] = s
        dst_max[:, g] = jnp.max(s, axis=0, keepdims=True)

    def softmax_pv(bufs, j, g):
        src, src_max = bufs
        off = pl.multiple_of(j * tk, tk)
        m_prev = m_scr[:, g]
        m_next = jnp.maximum(m_prev, src_max[:, g])
        alpha = jnp.exp2(m_prev - m_next)
        p = jnp.exp2(src[:, g] - m_next).astype(BF16)
        pv = jnp.dot(vT_ref[0, :, pl.ds(off, tk)], p, preferred_element_type=F32)
        acc_scr[:, g] = alpha * acc_scr[:, g] + pv
        m_scr[:, g] = m_next

    def chunk(j, parity, last):
        for g in groups:
            if not last:
                scores(j + 1, s_bufs[1 - parity], g)
            softmax_pv(s_bufs[parity], j, g)

    for g in groups:
        scores(0, s_bufs[0], g)
    n_trips = (nk - 1) // chunks_per_trip

    def body(t, carry):
        for k in range(chunks_per_trip):
            chunk(t * chunks_per_trip + k, k % 2, False)
        return carry

    if n_trips > 0:
        lax.fori_loop(0, n_trips, body, 0)
    for j in range(n_trips * chunks_per_trip, nk):
        chunk(j, j % 2, j == nk - 1)
    oT = acc_scr[:V_HEAD, :] / acc_scr[V_HEAD:V_HEAD + 1, :]
    o_ref[0] = oT.T.astype(BF16)


def _mla_attn(qT, k, vT, *, tq, tk, chunks_per_trip):
    b, s, _ = k.shape
    assert s % tk == 0 and s % tq == 0 and chunks_per_trip % 2 == 0
    grid = (b, MLA_HEADS, s // tq)
    return pl.pallas_call(
        functools.partial(_mla_attn_kernel, tk=tk, chunks_per_trip=chunks_per_trip),
        grid=grid,
        in_specs=[
            pl.BlockSpec((1, QK_PAD, tq), lambda bi, h, i: (bi, h, i)),
            pl.BlockSpec((1, s, QK_PAD), lambda bi, h, i: (bi, 0, h)),
            pl.BlockSpec((1, VT_ROWS, s), lambda bi, h, i: (bi, h, 0)),
        ],
        out_specs=pl.BlockSpec((1, tq, V_HEAD), lambda bi, h, i: (bi, i, h)),
        out_shape=jax.ShapeDtypeStruct((b, s, MLA_HEADS * V_HEAD), BF16),
        scratch_shapes=[pltpu.VMEM((tk, tq), F32), pltpu.VMEM((tk, tq), F32),
                        pltpu.VMEM((1, tq), F32), pltpu.VMEM((1, tq), F32),
                        pltpu.VMEM((1, tq), F32), pltpu.VMEM((VT_ROWS, tq), F32)],
        compiler_params=_params(3),
        name="mla_attn",
    )(qT, k, vT)


def _na_proj_kernel(x_ref, g_ref, w_ref, o_ref, *, q_scale):
    h = _rms(x_ref[0], g_ref[...]).astype(BF16)
    qkv = jnp.dot(h, w_ref[...], preferred_element_type=F32)
    o_ref[0, :, :D_MODEL] = (qkv[:, :D_MODEL] * q_scale).astype(BF16)
    o_ref[0, :, D_MODEL:] = qkv[:, D_MODEL:].astype(BF16)


def _na_proj(x, g, w_qkv, *, tm):
    b, s, _ = x.shape
    row = lambda bi, i: (bi, i, 0)
    return pl.pallas_call(
        functools.partial(_na_proj_kernel, q_scale=NA_HEAD_DIM ** -0.5),
        grid=(b, s // tm),
        in_specs=[pl.BlockSpec((1, tm, D_MODEL), row), _const_spec((1, D_MODEL)),
                  _const_spec((D_MODEL, 3 * D_MODEL))],
        out_specs=pl.BlockSpec((1, tm, 3 * D_MODEL), row),
        out_shape=jax.ShapeDtypeStruct((b, s, 3 * D_MODEL), BF16),
        compiler_params=_params(2),
        name="na_proj",
    )(x, g.reshape(1, -1), w_qkv.astype(BF16))


def _na_bias_table(rpb):
    c = np.arange(GRID_W)[:, None]
    kc = np.arange(GRID_W)[None, :]
    c0 = np.clip(c - NA_KW // 2, 0, GRID_W - NA_KW)
    valid = (kc >= c0) & (kc < c0 + NA_KW)
    rel_col = kc - c + NA_KW - 1
    n_rel = 2 * NA_KW - 1
    onehot = (valid[..., None] & (rel_col[..., None] == np.arange(n_rel))).astype(np.float32)
    band = jnp.einsum("hrm,ckm->hrck", rpb.astype(F32), jnp.asarray(onehot),
                      precision=lax.Precision.HIGHEST)
    band = jnp.where(jnp.asarray(valid), band, MASK_VALUE)
    tbl = jnp.stack([band[:, NA_KH - 1 - pat:2 * NA_KH - 1 - pat] for pat in range(NA_KH)])
    tbl = tbl.transpose(0, 1, 3, 2, 4)
    return tbl.reshape(NA_KH, NA_PAIRS, 2 * GRID_W, NA_WIN)


def _na_attn_kernel(q_ref, k_ref, v_ref, bias_ref, o_ref, *, rows_per_step, n_rows):
    rb = pl.program_id(2)
    lane = lax.broadcasted_iota(jnp.int32, (GRID_W, LANES), 1)
    first = lane < NA_HEAD_DIM

    def window(i):
        r = rb * rows_per_step + i
        r0 = jnp.clip(r - NA_KH // 2, 0, n_rows - NA_KH)
        return r - r0, pl.multiple_of(r0 * GRID_W, GRID_W)

    def scores(i):
        pat, koff = window(i)
        q2 = q_ref[0, i * GRID_W:(i + 1) * GRID_W, :]
        zero = jnp.zeros_like(q2)
        qs = jnp.concatenate([jnp.where(first, q2, zero), jnp.where(first, zero, q2)], axis=0)
        kw = k_ref[0, pl.ds(koff, NA_WIN), :]
        s = lax.dot_general(qs, kw, (((1,), (1,)), ((), ())), preferred_element_type=F32)
        return s + bias_ref[pat, 0]

    def finish(i, s):
        _, koff = window(i)
        vw = v_ref[0, pl.ds(koff, NA_WIN), :]
        m = jnp.max(s, axis=1, keepdims=True)
        p = jnp.exp(s - m)
        l = jnp.sum(p, axis=1, keepdims=True)
        pv = jnp.dot(p.astype(BF16), vw, preferred_element_type=F32) / l
        o = jnp.where(first, pv[:GRID_W], pv[GRID_W:])
        o_ref[0, i * GRID_W:(i + 1) * GRID_W, :] = o.astype(BF16)

    ahead = min(3, rows_per_step)
    pending = [scores(i) for i in range(ahead)]
    for i in range(rows_per_step):
        if i + ahead < rows_per_step:
            pending.append(scores(i + ahead))
        finish(i, pending[i])
        pending[i] = None


def _na_attn(qkv, bias, *, rows_per_step):
    b, s, _ = qkv.shape
    n_rows = s // GRID_W
    tq = rows_per_step * GRID_W
    grid = (NA_PAIRS, b, n_rows // rows_per_step)
    return pl.pallas_call(
        functools.partial(_na_attn_kernel, rows_per_step=rows_per_step, n_rows=n_rows),
        grid=grid,
        in_specs=[
            pl.BlockSpec((1, tq, LANES), lambda p, bi, i: (bi, i, p)),
            pl.BlockSpec((1, s, LANES), lambda p, bi, i: (bi, 0, NA_PAIRS + p)),
            pl.BlockSpec((1, s, LANES), lambda p, bi, i: (bi, 0, 2 * NA_PAIRS + p)),
            pl.BlockSpec((NA_KH, 1, 2 * GRID_W, NA_WIN), lambda p, bi, i: (0, p, 0, 0)),
        ],
        out_specs=pl.BlockSpec((1, tq, LANES), lambda p, bi, i: (bi, i, p)),
        out_shape=jax.ShapeDtypeStruct((b, s, D_MODEL), BF16),
        compiler_params=_params(3),
        name="na_attn",
    )(qkv, qkv, qkv, bias)


def _post_kernel(o_ref, x_ref, wo_ref, g_ref, w1_ref, w2_ref, gf_ref, y_ref, *, ff_chunk, final):
    x = x_ref[0] + jnp.dot(o_ref[0], wo_ref[...], preferred_element_type=F32)
    h = _rms(x, g_ref[...]).astype(BF16)
    y = x
    for c in range(D_FF // ff_chunk):
        a = jnp.dot(h, w1_ref[:, c * ff_chunk:(c + 1) * ff_chunk], preferred_element_type=F32)
        a = jnp.maximum(a, 0.0)
        y = y + jnp.dot((a * a).astype(BF16), w2_ref[c * ff_chunk:(c + 1) * ff_chunk, :],
                        preferred_element_type=F32)
    if final:
        y = _rms(y, gf_ref[...])
    y_ref[0] = y


def _post(o, x, w_o, g_mlp, w1, w2, g_final, *, tm, final):
    b, s, d_o = o.shape
    row = lambda bi, i: (bi, i, 0)
    return pl.pallas_call(
        functools.partial(_post_kernel, ff_chunk=1024, final=final),
        grid=(b, s // tm),
        in_specs=[
            pl.BlockSpec((1, tm, d_o), row),
            pl.BlockSpec((1, tm, D_MODEL), row),
            _const_spec((d_o, D_MODEL)),
            _const_spec((1, D_MODEL)),
            _const_spec((D_MODEL, D_FF)),
            _const_spec((D_FF, D_MODEL)),
            _const_spec((1, D_MODEL)),
        ],
        out_specs=pl.BlockSpec((1, tm, D_MODEL), row),
        out_shape=jax.ShapeDtypeStruct((b, s, D_MODEL), F32),
        compiler_params=_params(2),
        name="post_final" if final else "post",
    )(o, x, w_o.astype(BF16), g_mlp.reshape(1, -1), w1.astype(BF16), w2.astype(BF16),
      g_final.reshape(1, -1))


def _trunk(x, attn_norm, mlp_norm, final_norm, mla_w_dq, mla_q_norm, mla_w_uq, mla_w_dkv,
           mla_kv_norm, mla_w_ukv, mla_w_o, na_w_qkv, na_bias, na_w_o, mlp_w1, mlp_w2):
    s = x.shape[1]
    tm = min(512, s)
    qT, k, vT = _mla_proj(x, attn_norm[0], mla_w_dq[0], mla_q_norm[0], mla_w_uq[0], mla_w_dkv[0],
                          mla_kv_norm[0], mla_w_ukv[0], tm=tm)
    tq, tk = (1024, 2048) if s >= 8192 else (min(2048, s), min(512, s // 2))
    o = _mla_attn(qT, k, vT, tq=tq, tk=tk, chunks_per_trip=2)
    x = _post(o, x, mla_w_o[0], mlp_norm[0], mlp_w1[0], mlp_w2[0], final_norm, tm=tm, final=False)
    qkv = _na_proj(x, attn_norm[1], na_w_qkv[0], tm=tm)
    o = _na_attn(qkv, na_bias, rows_per_step=min(64, s // GRID_W))
    return _post(o, x, na_w_o[0], mlp_norm[1], mlp_w1[1], mlp_w2[1], final_norm, tm=tm, final=True)


def kernel(x_prompt, x_sample, attn_norm, mlp_norm, final_norm, mla_w_dq, mla_q_norm, mla_w_uq, mla_w_dkv, mla_kv_norm, mla_w_ukv, mla_w_o, na_w_qkv, na_rpb, na_w_o, mlp_w1, mlp_w2):
    na_bias = _na_bias_table(na_rpb[0])
    args = (attn_norm, mlp_norm, final_norm, mla_w_dq, mla_q_norm, mla_w_uq, mla_w_dkv, mla_kv_norm,
            mla_w_ukv, mla_w_o, na_w_qkv, na_bias, na_w_o, mlp_w1, mlp_w2)
    return (_trunk(x_prompt, *args), _trunk(x_sample, *args))
```

```python
import functools
import math

import numpy as np
import jax
import jax.numpy as jnp
from jax import lax
from jax.experimental import pallas as pl
from jax.experimental.pallas import tpu as pltpu

D_MODEL = 1024
GRID_W = 64
MLA_HEADS = 16
Q_LORA = 384
KV_LORA = 256
QK_NOPE = 128
QK_ROPE = 64
V_HEAD = 128
ROPE_THETA = 10000.0
NA_HEADS = 16
NA_HEAD_DIM = D_MODEL // NA_HEADS
NA_KH = 8
NA_KW = 16
D_FF = 4 * D_MODEL
EPS = 1e-6

LANES = 128
MXU_WIDTH = 256
BF16_ROWS = 16
QK_PAD = 2 * LANES
VT_ROWS = V_HEAD + BF16_ROWS
NA_PAIRS = NA_HEADS * NA_HEAD_DIM // LANES
NA_WIN = NA_KH * GRID_W
MASK_VALUE = -1e30
VMEM_LIMIT = 56 * 1024 * 1024

F32 = jnp.float32
BF16 = jnp.bfloat16


def _rms(x, g):
    return x * lax.rsqrt(jnp.mean(x * x, axis=-1, keepdims=True) + EPS) * g


def _const_spec(shape):
    zeros = (0,) * len(shape)
    return pl.BlockSpec(shape, lambda *_: zeros, pipeline_mode=pl.Buffered(1))


def _params(n_axes):
    return pltpu.CompilerParams(
        dimension_semantics=("arbitrary",) * n_axes, vmem_limit_bytes=VMEM_LIMIT)


def _mla_proj_kernel(x_ref, g_ref, wdq_ref, qn_ref, wuqT_ref, wdkvc_ref, wkr_ref, kvn_ref,
                     wuk_ref, wuvT_ref, cosT_ref, sinT_ref, tk0_ref, tk1_ref,
                     qT_ref, k_ref, vT_ref, *, q_scale):
    nt = (((1,), (1,)), ((), ()))
    half = QK_ROPE // 2
    d_qk = QK_NOPE + QK_ROPE
    h = _rms(x_ref[0], g_ref[...]).astype(BF16)
    tm = h.shape[0]
    cq = jnp.dot(h, wdq_ref[...], preferred_element_type=F32)
    cq = _rms(cq, qn_ref[...]).astype(BF16)
    qT = lax.dot_general(wuqT_ref[...], cq, nt, preferred_element_type=F32)
    cos = cosT_ref[...] * q_scale
    sin = sinT_ref[...] * q_scale
    zpad = jnp.zeros((QK_PAD - d_qk, tm), BF16)
    for hd in range(MLA_HEADS):
        src, dst = hd * d_qk, hd * QK_PAD
        x1 = qT[src + QK_NOPE:src + QK_NOPE + half]
        x2 = qT[src + QK_NOPE + half:src + d_qk]
        qT_ref[0, dst:dst + QK_NOPE, :] = (qT[src:src + QK_NOPE] * q_scale).astype(BF16)
        qT_ref[0, dst + QK_NOPE:dst + QK_NOPE + half, :] = (x1 * cos - x2 * sin).astype(BF16)
        qT_ref[0, dst + QK_NOPE + half:dst + d_qk, :] = (x1 * sin + x2 * cos).astype(BF16)
        qT_ref[0, dst + d_qk:dst + QK_PAD, :] = zpad

    ckv = jnp.dot(h, wdkvc_ref[...], preferred_element_type=F32)
    ckv = _rms(ckv, kvn_ref[...]).astype(BF16)
    kn = jnp.dot(ckv, wuk_ref[...], preferred_element_type=F32).astype(BF16)
    kr = jnp.dot(h, wkr_ref[...], preferred_element_type=F32)
    kr = (kr * tk0_ref[...] + pltpu.roll(kr, LANES // 2, axis=1) * tk1_ref[...]).astype(BF16)
    for hd in range(MLA_HEADS):
        k_ref[0, :, hd * QK_PAD:hd * QK_PAD + QK_NOPE] = kn[:, hd * QK_NOPE:(hd + 1) * QK_NOPE]
        k_ref[0, :, hd * QK_PAD + QK_NOPE:(hd + 1) * QK_PAD] = kr
    vT = lax.dot_general(wuvT_ref[...], ckv, nt, preferred_element_type=F32).astype(BF16)
    ones_row = lax.broadcasted_iota(jnp.int32, (VT_ROWS - V_HEAD, tm), 0) == 0
    tail = jnp.where(ones_row, 1.0, 0.0).astype(BF16)
    for hd in range(MLA_HEADS):
        vT_ref[0, hd * VT_ROWS:hd * VT_ROWS + V_HEAD, :] = vT[hd * V_HEAD:(hd + 1) * V_HEAD]
        vT_ref[0, hd * VT_ROWS + V_HEAD:(hd + 1) * VT_ROWS, :] = tail


def _mla_proj(x, g, w_dq, q_norm, w_uq, w_dkv, kv_norm, w_ukv, *, tm):
    b, s, _ = x.shape
    q_scale = (QK_NOPE + QK_ROPE) ** -0.5 * math.log2(math.e)

    half = QK_ROPE // 2
    w_uqT = w_uq.T.astype(BF16)
    wdkv_c = w_dkv[:, :KV_LORA].astype(BF16)
    k1, k2 = w_dkv[:, KV_LORA:KV_LORA + half], w_dkv[:, KV_LORA + half:]
    w_kr = jnp.concatenate([k1, k2, k2, k1], axis=-1).astype(BF16)
    wkv = w_ukv.reshape(KV_LORA, MLA_HEADS, QK_NOPE + V_HEAD)
    w_uk = wkv[..., :QK_NOPE].reshape(KV_LORA, MLA_HEADS * QK_NOPE).astype(BF16)
    w_uvT = wkv[..., QK_NOPE:].reshape(KV_LORA, MLA_HEADS * V_HEAD).T.astype(BF16)

    inv = ROPE_THETA ** (-jnp.arange(0, QK_ROPE, 2, dtype=F32) / QK_ROPE)
    ang = jnp.arange(s, dtype=F32)[:, None] * inv[None, :]
    cos, sin = jnp.cos(ang), jnp.sin(ang)
    zero = jnp.zeros_like(cos)
    t0 = jnp.concatenate([cos, cos, zero, zero], axis=-1)
    t1 = jnp.concatenate([-sin, sin, zero, zero], axis=-1)

    grid = (b, s // tm)
    row = lambda bi, i: (bi, i, 0)
    col = lambda bi, i: (bi, 0, i)
    tab = pl.BlockSpec((tm, LANES), lambda bi, i: (i, 0))
    tabT = pl.BlockSpec((half, tm), lambda bi, i: (0, i))
    return pl.pallas_call(
        functools.partial(_mla_proj_kernel, q_scale=q_scale),
        grid=grid,
        in_specs=[
            pl.BlockSpec((1, tm, D_MODEL), row),
            _const_spec((1, D_MODEL)),
            _const_spec((D_MODEL, Q_LORA)),
            _const_spec((1, Q_LORA)),
            _const_spec((MLA_HEADS * (QK_NOPE + QK_ROPE), Q_LORA)),
            _const_spec((D_MODEL, KV_LORA)),
            _const_spec((D_MODEL, LANES)),
            _const_spec((1, KV_LORA)),
            _const_spec((KV_LORA, MLA_HEADS * QK_NOPE)),
            _const_spec((MLA_HEADS * V_HEAD, KV_LORA)),
            tabT, tabT, tab, tab,
        ],
        out_specs=[
            pl.BlockSpec((1, MLA_HEADS * QK_PAD, tm), col),
            pl.BlockSpec((1, tm, MLA_HEADS * QK_PAD), row),
            pl.BlockSpec((1, MLA_HEADS * VT_ROWS, tm), col),
        ],
        out_shape=[
            jax.ShapeDtypeStruct((b, MLA_HEADS * QK_PAD, s), BF16),
            jax.ShapeDtypeStruct((b, s, MLA_HEADS * QK_PAD), BF16),
            jax.ShapeDtypeStruct((b, MLA_HEADS * VT_ROWS, s), BF16),
        ],
        compiler_params=_params(2),
        name="mla_proj",
    )(x, g.reshape(1, -1), w_dq.astype(BF16), q_norm.reshape(1, -1), w_uqT, wdkv_c, w_kr,
      kv_norm.reshape(1, -1), w_uk, w_uvT, cos.T, sin.T, t0, t1)


def _mla_attn_kernel(qT_ref, k_ref, vT_ref, o_ref, sa_scr, sb_scr, ca_scr, cb_scr, m_scr, acc_scr, *, tk,
                     chunks_per_trip):
    nk = k_ref.shape[1] // tk
    tq = qT_ref.shape[2]
    m_scr[...] = jnp.full(m_scr.shape, MASK_VALUE, F32)
    acc_scr[...] = jnp.zeros(acc_scr.shape, F32)
    s_bufs = ((sa_scr, ca_scr), (sb_scr, cb_scr))
    groups = [slice(c, c + MXU_WIDTH) for c in range(0, tq, MXU_WIDTH)]

    def scores(j, bufs, g):
        dst, dst_max = bufs
        off = pl.multiple_of(j * tk, tk)
        s = jnp.dot(k_ref[0, pl.ds(off, tk), :], qT_ref[0, :, g], preferred_element_type=F32)
        dst[:, g] = s
        dst_max[:, g] = jnp.max(s, axis=0, keepdims=True)

    def softmax_pv(bufs, j, g):
        src, src_max = bufs
        off = pl.multiple_of(j * tk, tk)
        m_prev = m_scr[:, g]
        m_next = jnp.maximum(m_prev, src_max[:, g])
        alpha = jnp.exp2(m_prev - m_next)
        p = jnp.exp2(src[:, g] - m_next).astype(BF16)
        pv = jnp.dot(vT_ref[0, :, pl.ds(off, tk)], p, preferred_element_type=F32)
        acc_scr[:, g] = alpha * acc_scr[:, g] + pv
        m_scr[:, g] = m_next

    def chunk(j, parity, last):
        for g in groups:
            if not last:
                scores(j + 1, s_bufs[1 - parity], g)
            softmax_pv(s_bufs[parity], j, g)

    for g in groups:
        scores(0, s_bufs[0], g)
    n_trips = (nk - 1) // chunks_per_trip

    def body(t, carry):
        for k in range(chunks_per_trip):
            chunk(t * chunks_per_trip + k, k % 2, False)
        return carry

    if n_trips > 0:
        lax.fori_loop(0, n_trips, body, 0)
    for j in range(n_trips * chunks_per_trip, nk):
        chunk(j, j % 2, j == nk - 1)
    oT = acc_scr[:V_HEAD, :] / acc_scr[V_HEAD:V_HEAD + 1, :]
    o_ref[0] = oT.T.astype(BF16)


def _mla_attn(qT, k, vT, *, tq, tk, chunks_per_trip):
    b, s, _ = k.shape
    assert s % tk == 0 and s % tq == 0 and chunks_per_trip % 2 == 0
    grid = (b, MLA_HEADS, s // tq)
    return pl.pallas_call(
        functools.partial(_mla_attn_kernel, tk=tk, chunks_per_trip=chunks_per_trip),
        grid=grid,
        in_specs=[
            pl.BlockSpec((1, QK_PAD, tq), lambda bi, h, i: (bi, h, i)),
            pl.BlockSpec((1, s, QK_PAD), lambda bi, h, i: (bi, 0, h)),
            pl.BlockSpec((1, VT_ROWS, s), lambda bi, h, i: (bi, h, 0)),
        ],
        out_specs=pl.BlockSpec((1, tq, V_HEAD), lambda bi, h, i: (bi, i, h)),
        out_shape=jax.ShapeDtypeStruct((b, s, MLA_HEADS * V_HEAD), BF16),
        scratch_shapes=[pltpu.VMEM((tk, tq), F32), pltpu.VMEM((tk, tq), F32),
                        pltpu.VMEM((1, tq), F32), pltpu.VMEM((1, tq), F32),
                        pltpu.VMEM((1, tq), F32), pltpu.VMEM((VT_ROWS, tq), F32)],
        compiler_params=_params(3),
        name="mla_attn",
    )(qT, k, vT)


def _na_proj_kernel(x_ref, g_ref, w_ref, o_ref, *, q_scale):
    h = _rms(x_ref[0], g_ref[...]).astype(BF16)
    qkv = jnp.dot(h, w_ref[...], preferred_element_type=F32)
    o_ref[0, :, :D_MODEL] = (qkv[:, :D_MODEL] * q_scale).astype(BF16)
    o_ref[0, :, D_MODEL:] = qkv[:, D_MODEL:].astype(BF16)


def _na_proj(x, g, w_qkv, *, tm):
    b, s, _ = x.shape
    row = lambda bi, i: (bi, i, 0)
    return pl.pallas_call(
        functools.partial(_na_proj_kernel, q_scale=NA_HEAD_DIM ** -0.5 * math.log2(math.e)),
        grid=(b, s // tm),
        in_specs=[pl.BlockSpec((1, tm, D_MODEL), row), _const_spec((1, D_MODEL)),
                  _const_spec((D_MODEL, 3 * D_MODEL))],
        out_specs=pl.BlockSpec((1, tm, 3 * D_MODEL), row),
        out_shape=jax.ShapeDtypeStruct((b, s, 3 * D_MODEL), BF16),
        compiler_params=_params(2),
        name="na_proj",
    )(x, g.reshape(1, -1), w_qkv.astype(BF16))


def _na_bias_table(rpb):
    c = np.arange(GRID_W)[:, None]
    kc = np.arange(GRID_W)[None, :]
    c0 = np.clip(c - NA_KW // 2, 0, GRID_W - NA_KW)
    valid = (kc >= c0) & (kc < c0 + NA_KW)
    rel_col = kc - c + NA_KW - 1
    n_rel = 2 * NA_KW - 1
    onehot = (valid[..., None] & (rel_col[..., None] == np.arange(n_rel))).astype(np.float32)
    band = jnp.einsum("hrm,ckm->hrck", rpb.astype(F32), jnp.asarray(onehot),
                      precision=lax.Precision.HIGHEST)
    band = jnp.where(jnp.asarray(valid), band * math.log2(math.e), MASK_VALUE)
    tbl = jnp.stack([band[:, NA_KH - 1 - pat:2 * NA_KH - 1 - pat] for pat in range(NA_KH)])
    tbl = tbl.transpose(0, 1, 3, 2, 4)
    return tbl.reshape(NA_KH, NA_PAIRS, 2 * GRID_W, NA_WIN)


def _na_attn_kernel(q_ref, k_ref, v_ref, bias_ref, o_ref, *, rows_per_step, n_rows):
    rb = pl.program_id(2)
    lane = lax.broadcasted_iota(jnp.int32, (GRID_W, LANES), 1)
    first = lane < NA_HEAD_DIM

    def window(i):
        r = rb * rows_per_step + i
        r0 = jnp.clip(r - NA_KH // 2, 0, n_rows - NA_KH)
        return r - r0, pl.multiple_of(r0 * GRID_W, GRID_W)

    def scores(i):
        pat, koff = window(i)
        q2 = q_ref[0, i * GRID_W:(i + 1) * GRID_W, :]
        zero = jnp.zeros_like(q2)
        qs = jnp.concatenate([jnp.where(first, q2, zero), jnp.where(first, zero, q2)], axis=0)
        kw = k_ref[0, pl.ds(koff, NA_WIN), :]
        s = lax.dot_general(qs, kw, (((1,), (1,)), ((), ())), preferred_element_type=F32)
        return s + bias_ref[pat, 0]

    def finish(i, s):
        _, koff = window(i)
        vw = v_ref[0, pl.ds(koff, NA_WIN), :]
        v_ext = jnp.concatenate([vw, jnp.ones_like(vw)], axis=1)
        m = jnp.max(s, axis=1, keepdims=True)
        p = jnp.exp2(s - m).astype(BF16)
        pv = jnp.dot(p, v_ext, preferred_element_type=F32)
        pv = pv[:, :LANES] / pv[:, LANES:]
        o = jnp.where(first, pv[:GRID_W], pv[GRID_W:])
        o_ref[0, i * GRID_W:(i + 1) * GRID_W, :] = o.astype(BF16)

    ahead = min(3, rows_per_step)
    pending = [scores(i) for i in range(ahead)]
    for i in range(rows_per_step):
        if i + ahead < rows_per_step:
            pending.append(scores(i + ahead))
        finish(i, pending[i])
        pending[i] = None


def _na_attn(qkv, bias, *, rows_per_step):
    b, s, _ = qkv.shape
    n_rows = s // GRID_W
    tq = rows_per_step * GRID_W
    grid = (NA_PAIRS, b, n_rows // rows_per_step)
    return pl.pallas_call(
        functools.partial(_na_attn_kernel, rows_per_step=rows_per_step, n_rows=n_rows),
        grid=grid,
        in_specs=[
            pl.BlockSpec((1, tq, LANES), lambda p, bi, i: (bi, i, p)),
            pl.BlockSpec((1, s, LANES), lambda p, bi, i: (bi, 0, NA_PAIRS + p)),
            pl.BlockSpec((1, s, LANES), lambda p, bi, i: (bi, 0, 2 * NA_PAIRS + p)),
            pl.BlockSpec((NA_KH, 1, 2 * GRID_W, NA_WIN), lambda p, bi, i: (0, p, 0, 0)),
        ],
        out_specs=pl.BlockSpec((1, tq, LANES), lambda p, bi, i: (bi, i, p)),
        out_shape=jax.ShapeDtypeStruct((b, s, D_MODEL), BF16),
        compiler_params=_params(3),
        name="na_attn",
    )(qkv, qkv, qkv, bias)


def _post_kernel(o_ref, x_ref, wo_ref, g_ref, w1_ref, w2_ref, gf_ref, y_ref, *, ff_chunk, final):
    x = x_ref[0] + jnp.dot(o_ref[0], wo_ref[...], preferred_element_type=F32)
    h = _rms(x, g_ref[...]).astype(BF16)
    y = x
    for c in range(D_FF // ff_chunk):
        a = jnp.dot(h, w1_ref[:, c * ff_chunk:(c + 1) * ff_chunk], preferred_element_type=F32)
        a = jnp.maximum(a, 0.0)
        y = y + jnp.dot((a * a).astype(BF16), w2_ref[c * ff_chunk:(c + 1) * ff_chunk, :],
                        preferred_element_type=F32)
    if final:
        y = _rms(y, gf_ref[...])
    y_ref[0] = y


def _post(o, x, w_o, g_mlp, w1, w2, g_final, *, tm, final):
    b, s, d_o = o.shape
    row = lambda bi, i: (bi, i, 0)
    return pl.pallas_call(
        functools.partial(_post_kernel, ff_chunk=1024, final=final),
        grid=(b, s // tm),
        in_specs=[
            pl.BlockSpec((1, tm, d_o), row),
            pl.BlockSpec((1, tm, D_MODEL), row),
            _const_spec((d_o, D_MODEL)),
            _const_spec((1, D_MODEL)),
            _const_spec((D_MODEL, D_FF)),
            _const_spec((D_FF, D_MODEL)),
            _const_spec((1, D_MODEL)),
        ],
        out_specs=pl.BlockSpec((1, tm, D_MODEL), row),
        out_shape=jax.ShapeDtypeStruct((b, s, D_MODEL), F32),
        compiler_params=_params(2),
        name="post_final" if final else "post",
    )(o, x, w_o.astype(BF16), g_mlp.reshape(1, -1), w1.astype(BF16), w2.astype(BF16),
      g_final.reshape(1, -1))


def _trunk(x, attn_norm, mlp_norm, final_norm, mla_w_dq, mla_q_norm, mla_w_uq, mla_w_dkv,
           mla_kv_norm, mla_w_ukv, mla_w_o, na_w_qkv, na_bias, na_w_o, mlp_w1, mlp_w2):
    s = x.shape[1]
    tm = min(512, s)
    qT, k, vT = _mla_proj(x, attn_norm[0], mla_w_dq[0], mla_q_norm[0], mla_w_uq[0], mla_w_dkv[0],
                          mla_kv_norm[0], mla_w_ukv[0], tm=tm)
    tq, tk = (1024, 1024) if s >= 8192 else (min(2048, s), min(512, s // 2))
    o = _mla_attn(qT, k, vT, tq=tq, tk=tk, chunks_per_trip=4 if s >= 8192 else 2)
    x = _post(o, x, mla_w_o[0], mlp_norm[0], mlp_w1[0], mlp_w2[0], final_norm, tm=tm, final=False)
    qkv = _na_proj(x, attn_norm[1], na_w_qkv[0], tm=tm)
    o = _na_attn(qkv, na_bias, rows_per_step=min(64, s // GRID_W))
    return _post(o, x, na_w_o[0], mlp_norm[1], mlp_w1[1], mlp_w2[1], final_norm, tm=tm, final=True)


def kernel(x_prompt, x_sample, attn_norm, mlp_norm, final_norm, mla_w_dq, mla_q_norm, mla_w_uq, mla_w_dkv, mla_kv_norm, mla_w_ukv, mla_w_o, na_w_qkv, na_rpb, na_w_o, mlp_w1, mlp_w2):
    na_bias = _na_bias_table(na_rpb[0])
    args = (attn_norm, mlp_norm, final_norm, mla_w_dq, mla_q_norm, mla_w_uq, mla_w_dkv, mla_kv_norm,
            mla_w_ukv, mla_w_o, na_w_qkv, na_bias, na_w_o, mlp_w1, mlp_w2)
    return (_trunk(x_prompt, *args), _trunk(x_sample, *args))
```

```python
import functools
import math

import numpy as np
import jax
import jax.numpy as jnp
from jax import lax
from jax.experimental import pallas as pl
from jax.experimental.pallas import tpu as pltpu

D_MODEL = 1024
GRID_W = 64
MLA_HEADS = 16
Q_LORA = 384
KV_LORA = 256
QK_NOPE = 128
QK_ROPE = 64
V_HEAD = 128
ROPE_THETA = 10000.0
NA_HEADS = 16
NA_HEAD_DIM = D_MODEL // NA_HEADS
NA_KH = 8
NA_KW = 16
D_FF = 4 * D_MODEL
EPS = 1e-6

LANES = 128
MXU_WIDTH = 256
BF16_ROWS = 16
QK_PAD = 2 * LANES
VT_ROWS = V_HEAD + BF16_ROWS
NA_PAIRS = NA_HEADS * NA_HEAD_DIM // LANES
NA_WIN = NA_KH * GRID_W
MASK_VALUE = -1e30
VMEM_LIMIT = 56 * 1024 * 1024

F32 = jnp.float32
BF16 = jnp.bfloat16


def _rms(x, g):
    return x * lax.rsqrt(jnp.mean(x * x, axis=-1, keepdims=True) + EPS) * g


def _const_spec(shape):
    zeros = (0,) * len(shape)
    return pl.BlockSpec(shape, lambda *_: zeros, pipeline_mode=pl.Buffered(1))


def _params(n_axes):
    return pltpu.CompilerParams(
        dimension_semantics=("arbitrary",) * n_axes, vmem_limit_bytes=VMEM_LIMIT)


def _mla_proj_kernel(x_ref, g_ref, wdq_ref, qn_ref, wuqT_ref, wdkvc_ref, wkr_ref, kvn_ref,
                     wuk_ref, wuvT_ref, cosT_ref, sinT_ref, tk0_ref, tk1_ref,
                     qT_ref, k_ref, vT_ref, *, q_scale):
    nt = (((1,), (1,)), ((), ()))
    half = QK_ROPE // 2
    d_qk = QK_NOPE + QK_ROPE
    h = _rms(x_ref[0], g_ref[...]).astype(BF16)
    tm = h.shape[0]
    cq = jnp.dot(h, wdq_ref[...], preferred_element_type=F32)
    cq = _rms(cq, qn_ref[...]).astype(BF16)
    qT = lax.dot_general(wuqT_ref[...], cq, nt, preferred_element_type=F32)
    cos = cosT_ref[...] * q_scale
    sin = sinT_ref[...] * q_scale
    zpad = jnp.zeros((QK_PAD - d_qk, tm), BF16)
    for hd in range(MLA_HEADS):
        src, dst = hd * d_qk, hd * QK_PAD
        x1 = qT[src + QK_NOPE:src + QK_NOPE + half]
        x2 = qT[src + QK_NOPE + half:src + d_qk]
        qT_ref[0, dst:dst + QK_NOPE, :] = (qT[src:src + QK_NOPE] * q_scale).astype(BF16)
        qT_ref[0, dst + QK_NOPE:dst + QK_NOPE + half, :] = (x1 * cos - x2 * sin).astype(BF16)
        qT_ref[0, dst + QK_NOPE + half:dst + d_qk, :] = (x1 * sin + x2 * cos).astype(BF16)
        qT_ref[0, dst + d_qk:dst + QK_PAD, :] = zpad

    ckv = jnp.dot(h, wdkvc_ref[...], preferred_element_type=F32)
    ckv = _rms(ckv, kvn_ref[...]).astype(BF16)
    kn = jnp.dot(ckv, wuk_ref[...], preferred_element_type=F32).astype(BF16)
    kr = jnp.dot(h, wkr_ref[...], preferred_element_type=F32)
    kr = (kr * tk0_ref[...] + pltpu.roll(kr, LANES // 2, axis=1) * tk1_ref[...]).astype(BF16)
    for hd in range(MLA_HEADS):
        k_ref[0, :, hd * QK_PAD:hd * QK_PAD + QK_NOPE] = kn[:, hd * QK_NOPE:(hd + 1) * QK_NOPE]
        k_ref[0, :, hd * QK_PAD + QK_NOPE:(hd + 1) * QK_PAD] = kr
    vT = lax.dot_general(wuvT_ref[...], ckv, nt, preferred_element_type=F32).astype(BF16)
    ones_row = lax.broadcasted_iota(jnp.int32, (VT_ROWS - V_HEAD, tm), 0) == 0
    tail = jnp.where(ones_row, 1.0, 0.0).astype(BF16)
    for hd in range(MLA_HEADS):
        vT_ref[0, hd * VT_ROWS:hd * VT_ROWS + V_HEAD, :] = vT[hd * V_HEAD:(hd + 1) * V_HEAD]
        vT_ref[0, hd * VT_ROWS + V_HEAD:(hd + 1) * VT_ROWS, :] = tail


def _mla_proj(x, g, w_dq, q_norm, w_uq, w_dkv, kv_norm, w_ukv, *, tm):
    b, s, _ = x.shape
    q_scale = (QK_NOPE + QK_ROPE) ** -0.5 * math.log2(math.e)

    half = QK_ROPE // 2
    w_uqT = w_uq.T.astype(BF16)
    wdkv_c = w_dkv[:, :KV_LORA].astype(BF16)
    k1, k2 = w_dkv[:, KV_LORA:KV_LORA + half], w_dkv[:, KV_LORA + half:]
    w_kr = jnp.concatenate([k1, k2, k2, k1], axis=-1).astype(BF16)
    wkv = w_ukv.reshape(KV_LORA, MLA_HEADS, QK_NOPE + V_HEAD)
    w_uk = wkv[..., :QK_NOPE].reshape(KV_LORA, MLA_HEADS * QK_NOPE).astype(BF16)
    w_uvT = wkv[..., QK_NOPE:].reshape(KV_LORA, MLA_HEADS * V_HEAD).T.astype(BF16)

    inv = ROPE_THETA ** (-jnp.arange(0, QK_ROPE, 2, dtype=F32) / QK_ROPE)
    ang = jnp.arange(s, dtype=F32)[:, None] * inv[None, :]
    cos, sin = jnp.cos(ang), jnp.sin(ang)
    zero = jnp.zeros_like(cos)
    t0 = jnp.concatenate([cos, cos, zero, zero], axis=-1)
    t1 = jnp.concatenate([-sin, sin, zero, zero], axis=-1)

    grid = (b, s // tm)
    row = lambda bi, i: (bi, i, 0)
    col = lambda bi, i: (bi, 0, i)
    tab = pl.BlockSpec((tm, LANES), lambda bi, i: (i, 0))
    tabT = pl.BlockSpec((half, tm), lambda bi, i: (0, i))
    return pl.pallas_call(
        functools.partial(_mla_proj_kernel, q_scale=q_scale),
        grid=grid,
        in_specs=[
            pl.BlockSpec((1, tm, D_MODEL), row),
            _const_spec((1, D_MODEL)),
            _const_spec((D_MODEL, Q_LORA)),
            _const_spec((1, Q_LORA)),
            _const_spec((MLA_HEADS * (QK_NOPE + QK_ROPE), Q_LORA)),
            _const_spec((D_MODEL, KV_LORA)),
            _const_spec((D_MODEL, LANES)),
            _const_spec((1, KV_LORA)),
            _const_spec((KV_LORA, MLA_HEADS * QK_NOPE)),
            _const_spec((MLA_HEADS * V_HEAD, KV_LORA)),
            tabT, tabT, tab, tab,
        ],
        out_specs=[
            pl.BlockSpec((1, MLA_HEADS * QK_PAD, tm), col),
            pl.BlockSpec((1, tm, MLA_HEADS * QK_PAD), row),
            pl.BlockSpec((1, MLA_HEADS * VT_ROWS, tm), col),
        ],
        out_shape=[
            jax.ShapeDtypeStruct((b, MLA_HEADS * QK_PAD, s), BF16),
            jax.ShapeDtypeStruct((b, s, MLA_HEADS * QK_PAD), BF16),
            jax.ShapeDtypeStruct((b, MLA_HEADS * VT_ROWS, s), BF16),
        ],
        compiler_params=_params(2),
        name="mla_proj",
    )(x, g.reshape(1, -1), w_dq.astype(BF16), q_norm.reshape(1, -1), w_uqT, wdkv_c, w_kr,
      kv_norm.reshape(1, -1), w_uk, w_uvT, cos.T, sin.T, t0, t1)


def _mla_attn_kernel(qT_ref, k_ref, vT_ref, o_ref, sa_scr, sb_scr, ca_scr, cb_scr, m_scr, acc_scr, *, tk,
                     chunks_per_trip):
    nk = k_ref.shape[1] // tk
    tq = qT_ref.shape[2]
    m_scr[...] = jnp.full(m_scr.shape, MASK_VALUE, F32)
    acc_scr[...] = jnp.zeros(acc_scr.shape, F32)
    s_bufs = ((sa_scr, ca_scr), (sb_scr, cb_scr))
    groups = [slice(c, c + MXU_WIDTH) for c in range(0, tq, MXU_WIDTH)]

    def scores(j, bufs, g):
        dst, dst_max = bufs
        off = pl.multiple_of(j * tk, tk)
        s = jnp.dot(k_ref[0, pl.ds(off, tk), :], qT_ref[0, :, g], preferred_element_type=F32)
        dst[:, g] = s
        dst_max[:, g] = jnp.max(s, axis=0, keepdims=True)

    def softmax_pv(bufs, j, g):
        src, src_max = bufs
        off = pl.multiple_of(j * tk, tk)
        m_prev = m_scr[:, g]
        m_next = jnp.maximum(m_prev, src_max[:, g])
        alpha = jnp.exp2(m_prev - m_next)
        p = jnp.exp2(src[:, g] - m_next).astype(BF16)
        pv = jnp.dot(vT_ref[0, :, pl.ds(off, tk)], p, preferred_element_type=F32)
        acc_scr[:, g] = alpha * acc_scr[:, g] + pv
        m_scr[:, g] = m_next

    def chunk(j, parity, last):
        for g in groups:
            if not last:
                scores(j + 1, s_bufs[1 - parity], g)
            softmax_pv(s_bufs[parity], j, g)

    for g in groups:
        scores(0, s_bufs[0], g)
    n_trips = (nk - 1) // chunks_per_trip

    def body(t, carry):
        for k in range(chunks_per_trip):
            chunk(t * chunks_per_trip + k, k % 2, False)
        return carry

    if n_trips > 0:
        lax.fori_loop(0, n_trips, body, 0)
    for j in range(n_trips * chunks_per_trip, nk):
        chunk(j, j % 2, j == nk - 1)
    oT = acc_scr[:V_HEAD, :] / acc_scr[V_HEAD:V_HEAD + 1, :]
    o_ref[0] = oT.T.astype(BF16)


def _mla_attn(qT, k, vT, *, tq, tk, chunks_per_trip):
    b, s, _ = k.shape
    assert s % tk == 0 and s % tq == 0 and chunks_per_trip % 2 == 0
    grid = (b, MLA_HEADS, s // tq)
    return pl.pallas_call(
        functools.partial(_mla_attn_kernel, tk=tk, chunks_per_trip=chunks_per_trip),
        grid=grid,
        in_specs=[
            pl.BlockSpec((1, QK_PAD, tq), lambda bi, h, i: (bi, h, i)),
            pl.BlockSpec((1, s, QK_PAD), lambda bi, h, i: (bi, 0, h)),
            pl.BlockSpec((1, VT_ROWS, s), lambda bi, h, i: (bi, h, 0)),
        ],
        out_specs=pl.BlockSpec((1, tq, V_HEAD), lambda bi, h, i: (bi, i, h)),
        out_shape=jax.ShapeDtypeStruct((b, s, MLA_HEADS * V_HEAD), BF16),
        scratch_shapes=[pltpu.VMEM((tk, tq), F32), pltpu.VMEM((tk, tq), F32),
                        pltpu.VMEM((1, tq), F32), pltpu.VMEM((1, tq), F32),
                        pltpu.VMEM((1, tq), F32), pltpu.VMEM((VT_ROWS, tq), F32)],
        compiler_params=_params(3),
        name="mla_attn",
    )(qT, k, vT)


def _na_bias_table(rpb):
    c = np.arange(GRID_W)[:, None]
    kc = np.arange(GRID_W)[None, :]
    c0 = np.clip(c - NA_KW // 2, 0, GRID_W - NA_KW)
    valid = (kc >= c0) & (kc < c0 + NA_KW)
    rel_col = kc - c + NA_KW - 1
    n_rel = 2 * NA_KW - 1
    onehot = (valid[..., None] & (rel_col[..., None] == np.arange(n_rel))).astype(np.float32)
    band = jnp.einsum("hrm,ckm->hrck", rpb.astype(F32), jnp.asarray(onehot),
                      precision=lax.Precision.HIGHEST)
    band = jnp.where(jnp.asarray(valid), band * math.log2(math.e), MASK_VALUE)
    tbl = jnp.stack([band[:, NA_KH - 1 - pat:2 * NA_KH - 1 - pat] for pat in range(NA_KH)])
    tbl = tbl.transpose(0, 1, 3, 2, 4)
    return tbl.reshape(NA_KH, NA_PAIRS, 2 * GRID_W, NA_WIN)


def _na_attn_kernel(q_ref, k_ref, v_ref, bias_ref, o_ref, *, rows_per_step, n_rows):
    rb = pl.program_id(2)
    lane = lax.broadcasted_iota(jnp.int32, (GRID_W, LANES), 1)
    first = lane < NA_HEAD_DIM

    def window(i):
        r = rb * rows_per_step + i
        r0 = jnp.clip(r - NA_KH // 2, 0, n_rows - NA_KH)
        return r - r0, pl.multiple_of(r0 * GRID_W, GRID_W)

    def scores(i):
        pat, koff = window(i)
        q2 = q_ref[0, i * GRID_W:(i + 1) * GRID_W, :]
        zero = jnp.zeros_like(q2)
        qs = jnp.concatenate([jnp.where(first, q2, zero), jnp.where(first, zero, q2)], axis=0)
        kw = k_ref[0, pl.ds(koff, NA_WIN), :]
        s = lax.dot_general(qs, kw, (((1,), (1,)), ((), ())), preferred_element_type=F32)
        return s + bias_ref[pat, 0]

    def finish(i, s):
        _, koff = window(i)
        vw = v_ref[0, pl.ds(koff, NA_WIN), :]
        v_ext = jnp.concatenate([vw, jnp.ones_like(vw)], axis=1)
        m = jnp.max(s, axis=1, keepdims=True)
        p = jnp.exp2(s - m).astype(BF16)
        pv = jnp.dot(p, v_ext, preferred_element_type=F32)
        pv = pv[:, :LANES] / pv[:, LANES:]
        o = jnp.where(first, pv[:GRID_W], pv[GRID_W:])
        o_ref[0, i * GRID_W:(i + 1) * GRID_W, :] = o.astype(BF16)

    ahead = min(3, rows_per_step)
    pending = [scores(i) for i in range(ahead)]
    for i in range(rows_per_step):
        if i + ahead < rows_per_step:
            pending.append(scores(i + ahead))
        finish(i, pending[i])
        pending[i] = None


def _na_attn(qkv, bias, *, rows_per_step):
    b, s, _ = qkv.shape
    n_rows = s // GRID_W
    tq = rows_per_step * GRID_W
    grid = (NA_PAIRS, b, n_rows // rows_per_step)
    return pl.pallas_call(
        functools.partial(_na_attn_kernel, rows_per_step=rows_per_step, n_rows=n_rows),
        grid=grid,
        in_specs=[
            pl.BlockSpec((1, tq, LANES), lambda p, bi, i: (bi, i, p)),
            pl.BlockSpec((1, s, LANES), lambda p, bi, i: (bi, 0, NA_PAIRS + p)),
            pl.BlockSpec((1, s, LANES), lambda p, bi, i: (bi, 0, 2 * NA_PAIRS + p)),
            pl.BlockSpec((NA_KH, 1, 2 * GRID_W, NA_WIN), lambda p, bi, i: (0, p, 0, 0)),
        ],
        out_specs=pl.BlockSpec((1, tq, LANES), lambda p, bi, i: (bi, i, p)),
        out_shape=jax.ShapeDtypeStruct((b, s, D_MODEL), BF16),
        compiler_params=_params(3),
        name="na_attn",
    )(qkv, qkv, qkv, bias)


def _post_kernel(o_ref, x_ref, wo_ref, g_ref, w1_ref, w2_ref, gn_ref, *rest, ff_chunk, final):
    x = x_ref[0] + jnp.dot(o_ref[0], wo_ref[...], preferred_element_type=F32)
    h = _rms(x, g_ref[...]).astype(BF16)
    y = x
    for c in range(D_FF // ff_chunk):
        a = jnp.dot(h, w1_ref[:, c * ff_chunk:(c + 1) * ff_chunk], preferred_element_type=F32)
        a = jnp.maximum(a, 0.0)
        y = y + jnp.dot((a * a).astype(BF16), w2_ref[c * ff_chunk:(c + 1) * ff_chunk, :],
                        preferred_element_type=F32)
    yn = _rms(y, gn_ref[...])
    if final:
        (y_ref,) = rest
        y_ref[0] = yn
    else:
        wqkv_ref, y_ref, qkv_ref = rest
        y_ref[0] = y
        qkv = jnp.dot(yn.astype(BF16), wqkv_ref[...], preferred_element_type=F32)
        q_scale = NA_HEAD_DIM ** -0.5 * math.log2(math.e)
        qkv_ref[0, :, :D_MODEL] = (qkv[:, :D_MODEL] * q_scale).astype(BF16)
        qkv_ref[0, :, D_MODEL:] = qkv[:, D_MODEL:].astype(BF16)


def _post(o, x, w_o, g_mlp, w1, w2, g_next, w_qkv=None, *, tm):
    final = w_qkv is None
    b, s, d_o = o.shape
    row = lambda bi, i: (bi, i, 0)
    in_specs = [
        pl.BlockSpec((1, tm, d_o), row),
        pl.BlockSpec((1, tm, D_MODEL), row),
        _const_spec((d_o, D_MODEL)),
        _const_spec((1, D_MODEL)),
        _const_spec((D_MODEL, D_FF)),
        _const_spec((D_FF, D_MODEL)),
        _const_spec((1, D_MODEL)),
    ]
    args = [o, x, w_o.astype(BF16), g_mlp.reshape(1, -1), w1.astype(BF16), w2.astype(BF16),
            g_next.reshape(1, -1)]
    out_specs = [pl.BlockSpec((1, tm, D_MODEL), row)]
    out_shape = [jax.ShapeDtypeStruct((b, s, D_MODEL), F32)]
    if not final:
        in_specs.append(_const_spec((D_MODEL, 3 * D_MODEL)))
        args.append(w_qkv.astype(BF16))
        out_specs.append(pl.BlockSpec((1, tm, 3 * D_MODEL), row))
        out_shape.append(jax.ShapeDtypeStruct((b, s, 3 * D_MODEL), BF16))
    out = pl.pallas_call(
        functools.partial(_post_kernel, ff_chunk=1024, final=final),
        grid=(b, s // tm),
        in_specs=in_specs,
        out_specs=out_specs,
        out_shape=out_shape,
        compiler_params=_params(2),
        name="post_final" if final else "post_qkv",
    )(*args)
    return out[0] if final else out


def _trunk(x, attn_norm, mlp_norm, final_norm, mla_w_dq, mla_q_norm, mla_w_uq, mla_w_dkv,
           mla_kv_norm, mla_w_ukv, mla_w_o, na_w_qkv, na_bias, na_w_o, mlp_w1, mlp_w2):
    s = x.shape[1]
    tm = min(512, s)
    qT, k, vT = _mla_proj(x, attn_norm[0], mla_w_dq[0], mla_q_norm[0], mla_w_uq[0], mla_w_dkv[0],
                          mla_kv_norm[0], mla_w_ukv[0], tm=tm)
    tq, tk = (1024, 1024) if s >= 8192 else (min(2048, s), min(512, s // 2))
    o = _mla_attn(qT, k, vT, tq=tq, tk=tk, chunks_per_trip=4 if s >= 8192 else 2)
    x, qkv = _post(o, x, mla_w_o[0], mlp_norm[0], mlp_w1[0], mlp_w2[0], attn_norm[1], na_w_qkv[0], tm=tm)
    o = _na_attn(qkv, na_bias, rows_per_step=min(64, s // GRID_W))
    return _post(o, x, na_w_o[0], mlp_norm[1], mlp_w1[1], mlp_w2[1], final_norm, tm=tm)


def kernel(x_prompt, x_sample, attn_norm, mlp_norm, final_norm, mla_w_dq, mla_q_norm, mla_w_uq, mla_w_dkv, mla_kv_norm, mla_w_ukv, mla_w_o, na_w_qkv, na_rpb, na_w_o, mlp_w1, mlp_w2):
    na_bias = _na_bias_table(na_rpb[0])
    args = (attn_norm, mlp_norm, final_norm, mla_w_dq, mla_q_norm, mla_w_uq, mla_w_dkv, mla_kv_norm,
            mla_w_ukv, mla_w_o, na_w_qkv, na_bias, na_w_o, mlp_w1, mlp_w2)
    return (_trunk(x_prompt, *args), _trunk(x_sample, *args))
```

```python
import functools
import math

import numpy as np
import jax
import jax.numpy as jnp
from jax import lax
from jax.experimental import pallas as pl
from jax.experimental.pallas import tpu as pltpu

D_MODEL = 1024
GRID_W = 64
MLA_HEADS = 16
Q_LORA = 384
KV_LORA = 256
QK_NOPE = 128
QK_ROPE = 64
V_HEAD = 128
ROPE_THETA = 10000.0
NA_HEADS = 16
NA_HEAD_DIM = D_MODEL // NA_HEADS
NA_KH = 8
NA_KW = 16
D_FF = 4 * D_MODEL
EPS = 1e-6

LANES = 128
MXU_WIDTH = 256
BF16_ROWS = 16
QK_PAD = 2 * LANES
VT_ROWS = V_HEAD + BF16_ROWS
NA_PAIRS = NA_HEADS * NA_HEAD_DIM // LANES
NA_WIN = NA_KH * GRID_W
MASK_VALUE = -1e30
VMEM_LIMIT = 56 * 1024 * 1024

F32 = jnp.float32
BF16 = jnp.bfloat16


def _rms(x, g):
    return x * lax.rsqrt(jnp.mean(x * x, axis=-1, keepdims=True) + EPS) * g


def _const_spec(shape):
    zeros = (0,) * len(shape)
    return pl.BlockSpec(shape, lambda *_: zeros, pipeline_mode=pl.Buffered(1))


def _params(n_axes):
    return pltpu.CompilerParams(
        dimension_semantics=("arbitrary",) * n_axes, vmem_limit_bytes=VMEM_LIMIT)


def _mla_proj_kernel(x_ref, g_ref, wdq_ref, qn_ref, wuqT_ref, wdkvc_ref, wkr_ref, kvn_ref,
                     wuk_ref, wuvT_ref, cosT_ref, sinT_ref, tk0_ref, tk1_ref,
                     qT_ref, k_ref, vT_ref, *, q_scale):
    nt = (((1,), (1,)), ((), ()))
    half = QK_ROPE // 2
    d_qk = QK_NOPE + QK_ROPE
    h = _rms(x_ref[0], g_ref[...]).astype(BF16)
    tm = h.shape[0]
    cq = jnp.dot(h, wdq_ref[...], preferred_element_type=F32)
    cq = _rms(cq, qn_ref[...]).astype(BF16)
    qT = lax.dot_general(wuqT_ref[...], cq, nt, preferred_element_type=F32)
    cos = cosT_ref[...] * q_scale
    sin = sinT_ref[...] * q_scale
    zpad = jnp.zeros((QK_PAD - d_qk, tm), BF16)
    for hd in range(MLA_HEADS):
        src, dst = hd * d_qk, hd * QK_PAD
        x1 = qT[src + QK_NOPE:src + QK_NOPE + half]
        x2 = qT[src + QK_NOPE + half:src + d_qk]
        qT_ref[0, dst:dst + QK_NOPE, :] = (qT[src:src + QK_NOPE] * q_scale).astype(BF16)
        qT_ref[0, dst + QK_NOPE:dst + QK_NOPE + half, :] = (x1 * cos - x2 * sin).astype(BF16)
        qT_ref[0, dst + QK_NOPE + half:dst + d_qk, :] = (x1 * sin + x2 * cos).astype(BF16)
        qT_ref[0, dst + d_qk:dst + QK_PAD, :] = zpad

    ckv = jnp.dot(h, wdkvc_ref[...], preferred_element_type=F32)
    ckv = _rms(ckv, kvn_ref[...]).astype(BF16)
    kn = jnp.dot(ckv, wuk_ref[...], preferred_element_type=F32).astype(BF16)
    kr = jnp.dot(h, wkr_ref[...], preferred_element_type=F32)
    kr = (kr * tk0_ref[...] + pltpu.roll(kr, LANES // 2, axis=1) * tk1_ref[...]).astype(BF16)
    for hd in range(MLA_HEADS):
        k_ref[0, :, hd * QK_PAD:hd * QK_PAD + QK_NOPE] = kn[:, hd * QK_NOPE:(hd + 1) * QK_NOPE]
        k_ref[0, :, hd * QK_PAD + QK_NOPE:(hd + 1) * QK_PAD] = kr
    vT = lax.dot_general(wuvT_ref[...], ckv, nt, preferred_element_type=F32).astype(BF16)
    ones_row = lax.broadcasted_iota(jnp.int32, (VT_ROWS - V_HEAD, tm), 0) == 0
    tail = jnp.where(ones_row, 1.0, 0.0).astype(BF16)
    for hd in range(MLA_HEADS):
        vT_ref[0, hd * VT_ROWS:hd * VT_ROWS + V_HEAD, :] = vT[hd * V_HEAD:(hd + 1) * V_HEAD]
        vT_ref[0, hd * VT_ROWS + V_HEAD:(hd + 1) * VT_ROWS, :] = tail


def _mla_proj(x, g, w_dq, q_norm, w_uq, w_dkv, kv_norm, w_ukv, *, tm):
    b, s, _ = x.shape
    q_scale = (QK_NOPE + QK_ROPE) ** -0.5 * math.log2(math.e)

    half = QK_ROPE // 2
    w_uqT = w_uq.T.astype(BF16)
    wdkv_c = w_dkv[:, :KV_LORA].astype(BF16)
    k1, k2 = w_dkv[:, KV_LORA:KV_LORA + half], w_dkv[:, KV_LORA + half:]
    w_kr = jnp.concatenate([k1, k2, k2, k1], axis=-1).astype(BF16)
    wkv = w_ukv.reshape(KV_LORA, MLA_HEADS, QK_NOPE + V_HEAD)
    w_uk = wkv[..., :QK_NOPE].reshape(KV_LORA, MLA_HEADS * QK_NOPE).astype(BF16)
    w_uvT = wkv[..., QK_NOPE:].reshape(KV_LORA, MLA_HEADS * V_HEAD).T.astype(BF16)

    inv = ROPE_THETA ** (-jnp.arange(0, QK_ROPE, 2, dtype=F32) / QK_ROPE)
    ang = jnp.arange(s, dtype=F32)[:, None] * inv[None, :]
    cos, sin = jnp.cos(ang), jnp.sin(ang)
    zero = jnp.zeros_like(cos)
    t0 = jnp.concatenate([cos, cos, zero, zero], axis=-1)
    t1 = jnp.concatenate([-sin, sin, zero, zero], axis=-1)

    grid = (b, s // tm)
    row = lambda bi, i: (bi, i, 0)
    col = lambda bi, i: (bi, 0, i)
    tab = pl.BlockSpec((tm, LANES), lambda bi, i: (i, 0))
    tabT = pl.BlockSpec((half, tm), lambda bi, i: (0, i))
    return pl.pallas_call(
        functools.partial(_mla_proj_kernel, q_scale=q_scale),
        grid=grid,
        in_specs=[
            pl.BlockSpec((1, tm, D_MODEL), row),
            _const_spec((1, D_MODEL)),
            _const_spec((D_MODEL, Q_LORA)),
            _const_spec((1, Q_LORA)),
            _const_spec((MLA_HEADS * (QK_NOPE + QK_ROPE), Q_LORA)),
            _const_spec((D_MODEL, KV_LORA)),
            _const_spec((D_MODEL, LANES)),
            _const_spec((1, KV_LORA)),
            _const_spec((KV_LORA, MLA_HEADS * QK_NOPE)),
            _const_spec((MLA_HEADS * V_HEAD, KV_LORA)),
            tabT, tabT, tab, tab,
        ],
        out_specs=[
            pl.BlockSpec((1, MLA_HEADS * QK_PAD, tm), col),
            pl.BlockSpec((1, tm, MLA_HEADS * QK_PAD), row),
            pl.BlockSpec((1, MLA_HEADS * VT_ROWS, tm), col),
        ],
        out_shape=[
            jax.ShapeDtypeStruct((b, MLA_HEADS * QK_PAD, s), BF16),
            jax.ShapeDtypeStruct((b, s, MLA_HEADS * QK_PAD), BF16),
            jax.ShapeDtypeStruct((b, MLA_HEADS * VT_ROWS, s), BF16),
        ],
        compiler_params=_params(2),
        name="mla_proj",
    )(x, g.reshape(1, -1), w_dq.astype(BF16), q_norm.reshape(1, -1), w_uqT, wdkv_c, w_kr,
      kv_norm.reshape(1, -1), w_uk, w_uvT, cos.T, sin.T, t0, t1)


def _mla_attn_kernel(qT_ref, k_ref, vT_ref, o_ref, sa_scr, sb_scr, ca_scr, cb_scr, m_scr, acc_scr, *, tk,
                     chunks_per_trip):
    nk = k_ref.shape[1] // tk
    tq = qT_ref.shape[2]
    m_scr[...] = jnp.full(m_scr.shape, MASK_VALUE, F32)
    acc_scr[...] = jnp.zeros(acc_scr.shape, F32)
    s_bufs = ((sa_scr, ca_scr), (sb_scr, cb_scr))
    groups = [slice(c, c + MXU_WIDTH) for c in range(0, tq, MXU_WIDTH)]

    def scores(j, bufs, g):
        dst, dst_max = bufs
        off = pl.multiple_of(j * tk, tk)
        s = jnp.dot(k_ref[0, pl.ds(off, tk), :], qT_ref[0, :, g], preferred_element_type=F32)
        dst[:, g] = s
        dst_max[:, g] = jnp.max(s, axis=0, keepdims=True)

    def softmax_pv(bufs, j, g):
        src, src_max = bufs
        off = pl.multiple_of(j * tk, tk)
        m_prev = m_scr[:, g]
        m_next = jnp.maximum(m_prev, src_max[:, g])
        alpha = jnp.exp2(m_prev - m_next)
        p = jnp.exp2(src[:, g] - m_next).astype(BF16)
        pv = jnp.dot(vT_ref[0, :, pl.ds(off, tk)], p, preferred_element_type=F32)
        acc_scr[:, g] = alpha * acc_scr[:, g] + pv
        m_scr[:, g] = m_next

    def chunk(j, parity, last):
        for g in groups:
            if not last:
                scores(j + 1, s_bufs[1 - parity], g)
            softmax_pv(s_bufs[parity], j, g)

    for g in groups:
        scores(0, s_bufs[0], g)
    n_trips = (nk - 1) // chunks_per_trip

    def body(t, carry):
        for k in range(chunks_per_trip):
            chunk(t * chunks_per_trip + k, k % 2, False)
        return carry

    if n_trips > 0:
        lax.fori_loop(0, n_trips, body, 0)
    for j in range(n_trips * chunks_per_trip, nk):
        chunk(j, j % 2, j == nk - 1)
    oT = acc_scr[:V_HEAD, :] / acc_scr[V_HEAD:V_HEAD + 1, :]
    o_ref[0] = oT.T.astype(BF16)


def _mla_attn(qT, k, vT, *, tq, tk, chunks_per_trip):
    b, s, _ = k.shape
    assert s % tk == 0 and s % tq == 0 and chunks_per_trip % 2 == 0
    grid = (b, MLA_HEADS, s // tq)
    return pl.pallas_call(
        functools.partial(_mla_attn_kernel, tk=tk, chunks_per_trip=chunks_per_trip),
        grid=grid,
        in_specs=[
            pl.BlockSpec((1, QK_PAD, tq), lambda bi, h, i: (bi, h, i)),
            pl.BlockSpec((1, s, QK_PAD), lambda bi, h, i: (bi, 0, h)),
            pl.BlockSpec((1, VT_ROWS, s), lambda bi, h, i: (bi, h, 0)),
        ],
        out_specs=pl.BlockSpec((1, tq, V_HEAD), lambda bi, h, i: (bi, i, h)),
        out_shape=jax.ShapeDtypeStruct((b, s, MLA_HEADS * V_HEAD), BF16),
        scratch_shapes=[pltpu.VMEM((tk, tq), F32), pltpu.VMEM((tk, tq), F32),
                        pltpu.VMEM((1, tq), F32), pltpu.VMEM((1, tq), F32),
                        pltpu.VMEM((1, tq), F32), pltpu.VMEM((VT_ROWS, tq), F32)],
        compiler_params=_params(3),
        name="mla_attn",
    )(qT, k, vT)


def _na_bias_table(rpb):
    c = np.arange(GRID_W)[:, None]
    kc = np.arange(GRID_W)[None, :]
    c0 = np.clip(c - NA_KW // 2, 0, GRID_W - NA_KW)
    valid = (kc >= c0) & (kc < c0 + NA_KW)
    rel_col = kc - c + NA_KW - 1
    n_rel = 2 * NA_KW - 1
    onehot = (valid[..., None] & (rel_col[..., None] == np.arange(n_rel))).astype(np.float32)
    band = jnp.einsum("hrm,ckm->hrck", rpb.astype(F32), jnp.asarray(onehot),
                      precision=lax.Precision.HIGHEST)
    band = jnp.where(jnp.asarray(valid), band * math.log2(math.e), MASK_VALUE)
    tbl = jnp.stack([band[:, NA_KH - 1 - pat:2 * NA_KH - 1 - pat] for pat in range(NA_KH)])
    tbl = tbl.transpose(0, 1, 3, 2, 4)
    return tbl.reshape(NA_KH, NA_PAIRS, 2 * GRID_W, NA_WIN)


def _na_attn_kernel(q_ref, k_ref, v_ref, bias_ref, o_ref, *, rows_per_step, n_rows):
    rb = pl.program_id(2)
    lane = lax.broadcasted_iota(jnp.int32, (GRID_W, LANES), 1)
    first = lane < NA_HEAD_DIM

    def window(i):
        r = rb * rows_per_step + i
        r0 = jnp.clip(r - NA_KH // 2, 0, n_rows - NA_KH)
        return r - r0, pl.multiple_of(r0 * GRID_W, GRID_W)

    def scores(i):
        pat, koff = window(i)
        q2 = q_ref[0, 0, i * GRID_W:(i + 1) * GRID_W, :]
        zero = jnp.zeros_like(q2)
        qs = jnp.concatenate([jnp.where(first, q2, zero), jnp.where(first, zero, q2)], axis=0)
        kw = k_ref[0, 0, pl.ds(koff, NA_WIN), :]
        s = lax.dot_general(qs, kw, (((1,), (1,)), ((), ())), preferred_element_type=F32)
        return s + bias_ref[pat, 0]

    def finish(i, s):
        _, koff = window(i)
        vw = v_ref[0, 0, pl.ds(koff, NA_WIN), :]
        v_ext = jnp.concatenate([vw, jnp.ones_like(vw)], axis=1)
        m = jnp.max(s, axis=1, keepdims=True)
        p = jnp.exp2(s - m).astype(BF16)
        pv = jnp.dot(p, v_ext, preferred_element_type=F32)
        pv = pv[:, :LANES] / pv[:, LANES:]
        o = jnp.where(first, pv[:GRID_W], pv[GRID_W:])
        o_ref[0, i * GRID_W:(i + 1) * GRID_W, :] = o.astype(BF16)

    ahead = min(3, rows_per_step)
    pending = [scores(i) for i in range(ahead)]
    for i in range(rows_per_step):
        if i + ahead < rows_per_step:
            pending.append(scores(i + ahead))
        finish(i, pending[i])
        pending[i] = None


def _na_attn(qkv, bias, *, rows_per_step):
    b, _, s, _ = qkv.shape
    n_rows = s // GRID_W
    tq = rows_per_step * GRID_W
    grid = (NA_PAIRS, b, n_rows // rows_per_step)
    return pl.pallas_call(
        functools.partial(_na_attn_kernel, rows_per_step=rows_per_step, n_rows=n_rows),
        grid=grid,
        in_specs=[
            pl.BlockSpec((1, 1, tq, LANES), lambda p, bi, i: (bi, p, i, 0)),
            pl.BlockSpec((1, 1, s, LANES), lambda p, bi, i: (bi, NA_PAIRS + p, 0, 0)),
            pl.BlockSpec((1, 1, s, LANES), lambda p, bi, i: (bi, 2 * NA_PAIRS + p, 0, 0)),
            pl.BlockSpec((NA_KH, 1, 2 * GRID_W, NA_WIN), lambda p, bi, i: (0, p, 0, 0)),
        ],
        out_specs=pl.BlockSpec((1, tq, LANES), lambda p, bi, i: (bi, i, p)),
        out_shape=jax.ShapeDtypeStruct((b, s, D_MODEL), BF16),
        compiler_params=_params(3),
        name="na_attn",
    )(qkv, qkv, qkv, bias)


def _post_kernel(o_ref, x_ref, wo_ref, g_ref, w1_ref, w2_ref, gn_ref, *rest, ff_chunk, final):
    x = x_ref[0] + jnp.dot(o_ref[0], wo_ref[...], preferred_element_type=F32)
    h = _rms(x, g_ref[...]).astype(BF16)
    y = x
    for c in range(D_FF // ff_chunk):
        a = jnp.dot(h, w1_ref[:, c * ff_chunk:(c + 1) * ff_chunk], preferred_element_type=F32)
        a = jnp.maximum(a, 0.0)
        y = y + jnp.dot((a * a).astype(BF16), w2_ref[c * ff_chunk:(c + 1) * ff_chunk, :],
                        preferred_element_type=F32)
    yn = _rms(y, gn_ref[...])
    if final:
        (y_ref,) = rest
        y_ref[0] = yn
    else:
        wqkv_ref, y_ref, qkv_ref = rest
        y_ref[0] = y
        qkv = jnp.dot(yn.astype(BF16), wqkv_ref[...], preferred_element_type=F32)
        q_scale = NA_HEAD_DIM ** -0.5 * math.log2(math.e)
        for c in range(3 * NA_PAIRS):
            col = qkv[:, c * LANES:(c + 1) * LANES]
            qkv_ref[0, c] = (col * q_scale if c < NA_PAIRS else col).astype(BF16)


def _post(o, x, w_o, g_mlp, w1, w2, g_next, w_qkv=None, *, tm):
    final = w_qkv is None
    b, s, d_o = o.shape
    row = lambda bi, i: (bi, i, 0)
    in_specs = [
        pl.BlockSpec((1, tm, d_o), row),
        pl.BlockSpec((1, tm, D_MODEL), row),
        _const_spec((d_o, D_MODEL)),
        _const_spec((1, D_MODEL)),
        _const_spec((D_MODEL, D_FF)),
        _const_spec((D_FF, D_MODEL)),
        _const_spec((1, D_MODEL)),
    ]
    args = [o, x, w_o.astype(BF16), g_mlp.reshape(1, -1), w1.astype(BF16), w2.astype(BF16),
            g_next.reshape(1, -1)]
    out_specs = [pl.BlockSpec((1, tm, D_MODEL), row)]
    out_shape = [jax.ShapeDtypeStruct((b, s, D_MODEL), F32)]
    if not final:
        in_specs.append(_const_spec((D_MODEL, 3 * D_MODEL)))
        args.append(w_qkv.astype(BF16))
        out_specs.append(pl.BlockSpec((1, 3 * NA_PAIRS, tm, LANES), lambda bi, i: (bi, 0, i, 0)))
        out_shape.append(jax.ShapeDtypeStruct((b, 3 * NA_PAIRS, s, LANES), BF16))
    out = pl.pallas_call(
        functools.partial(_post_kernel, ff_chunk=1024, final=final),
        grid=(b, s // tm),
        in_specs=in_specs,
        out_specs=out_specs,
        out_shape=out_shape,
        compiler_params=_params(2),
        name="post_final" if final else "post_qkv",
    )(*args)
    return out[0] if final else out


def _trunk(x, attn_norm, mlp_norm, final_norm, mla_w_dq, mla_q_norm, mla_w_uq, mla_w_dkv,
           mla_kv_norm, mla_w_ukv, mla_w_o, na_w_qkv, na_bias, na_w_o, mlp_w1, mlp_w2):
    s = x.shape[1]
    tm = min(512, s)
    qT, k, vT = _mla_proj(x, attn_norm[0], mla_w_dq[0], mla_q_norm[0], mla_w_uq[0], mla_w_dkv[0],
                          mla_kv_norm[0], mla_w_ukv[0], tm=tm)
    tq, tk = (1024, 1024) if s >= 8192 else (min(2048, s), min(512, s // 2))
    o = _mla_attn(qT, k, vT, tq=tq, tk=tk, chunks_per_trip=4 if s >= 8192 else 2)
    x, qkv = _post(o, x, mla_w_o[0], mlp_norm[0], mlp_w1[0], mlp_w2[0], attn_norm[1], na_w_qkv[0], tm=tm)
    o = _na_attn(qkv, na_bias, rows_per_step=min(64, s // GRID_W))
    return _post(o, x, na_w_o[0], mlp_norm[1], mlp_w1[1], mlp_w2[1], final_norm, tm=tm)


def kernel(x_prompt, x_sample, attn_norm, mlp_norm, final_norm, mla_w_dq, mla_q_norm, mla_w_uq, mla_w_dkv, mla_kv_norm, mla_w_ukv, mla_w_o, na_w_qkv, na_rpb, na_w_o, mlp_w1, mlp_w2):
    na_bias = _na_bias_table(na_rpb[0])
    args = (attn_norm, mlp_norm, final_norm, mla_w_dq, mla_q_norm, mla_w_uq, mla_w_dkv, mla_kv_norm,
            mla_w_ukv, mla_w_o, na_w_qkv, na_bias, na_w_o, mlp_w1, mlp_w2)
    return (_trunk(x_prompt, *args), _trunk(x_sample, *args))
```
